```python
import jax, jax.numpy as jnp
from jax import lax
import numpy as np

D_MODEL = 1024
BATCH = 2
SEQ = 16384
DEPTH = 1
DEC_BATCH = 32
DEC_SEQ = 16
PAST_LEN = 2048

CHUNK = 64
Q_BLOCK = 128
EPS = 1e-6
ROPE_BASE = 10000.0
MLA_HEADS = 8
MLA_NOPE = 64
MLA_ROPE = 32
MLA_QK = MLA_NOPE + MLA_ROPE
MLA_V = 64
MLA_Q_RANK = 256
MLA_KV_RANK = 128
MLA_OUT = MLA_HEADS * MLA_V
RET_HEADS = 4
RET_DK = 128
RET_DV = 128
RET_OUT = RET_HEADS * RET_DV
D_MIX = MLA_OUT + RET_OUT
O_KV = MLA_Q_RANK
O_KR = O_KV + MLA_KV_RANK
O_RQ = O_KR + MLA_ROPE
O_RK = O_RQ + RET_HEADS * RET_DK
O_RV = O_RK + RET_HEADS * RET_DK
O_RG = O_RV + RET_OUT
IN_COLS = O_RG + RET_OUT
MEM_TOKENS = 256
MEM_HEADS = 4
MEM_DH = D_MODEL // MEM_HEADS
D_FF = 2816
CONV_W = 3

kernel_name = 'hybrid_mla_retention_stream_step'


def rmsnorm(x, g):
    xf = x.astype(jnp.float32)
    y = xf * lax.rsqrt(jnp.mean(xf * xf, axis=-1, keepdims=True) + EPS)
    return (y * g.astype(jnp.float32)).astype(x.dtype)


def rope(x, pos, inv_freq):
    ang = pos.astype(jnp.float32)[:, None] * inv_freq[None, :]
    cos = jnp.cos(ang)[None, :, None, :]
    sin = jnp.sin(ang)[None, :, None, :]
    x1, x2 = jnp.split(x.astype(jnp.float32), 2, axis=-1)
    return jnp.concatenate([x1 * cos - x2 * sin, x2 * cos + x1 * sin], -1).astype(x.dtype)


def mla_freqs():
    return 1.0 / (ROPE_BASE ** (jnp.arange(0, MLA_ROPE, 2, dtype=jnp.float32) / MLA_ROPE))


def ret_freqs():
    return 1.0 / (ROPE_BASE ** jnp.linspace(0.0, 1.0, RET_DK // 2, dtype=jnp.float32))


def retention_log_decay():
    return jnp.log(1.0 - 2.0 ** (-5.0 - jnp.arange(RET_HEADS, dtype=jnp.float32)))


def mla_attend(q, pos_q, k, v, pos_k):
    s = jnp.einsum('bqhd,bkhd->bhqk', q, k).astype(jnp.float32) * (MLA_QK ** -0.5)
    allowed = (pos_k[None, :] // CHUNK) <= (pos_q[:, None] // CHUNK)
    s = jnp.where(allowed[None, None], s, -1e30)
    p = jax.nn.softmax(s, axis=-1).astype(v.dtype)
    return jnp.einsum('bhqk,bkhd->bqhd', p, v)


def retention_chunk(s_prev, q, k, v, log_g):
    L = q.shape[1]
    qf, kf, vf = q.astype(jnp.float32), k.astype(jnp.float32), v.astype(jnp.float32)
    idx = jnp.arange(L, dtype=jnp.float32)
    diff = idx[:, None] - idx[None, :]
    dmask = jnp.where(diff >= 0, jnp.exp(log_g[:, None, None] * jnp.maximum(diff, 0.0)), 0.0)
    inner = jnp.einsum('blhd,bmhd->bhlm', qf, kf) * dmask[None]
    o = jnp.einsum('bhlm,bmhe->blhe', inner, vf)
    q_dec = jnp.exp(log_g[None, :] * (idx[:, None] + 1.0))
    o = o + jnp.einsum('blhd,bhde->blhe', qf * q_dec[None, :, :, None], s_prev)
    k_dec = jnp.exp(log_g[None, :] * (L - 1.0 - idx)[:, None])
    s_new = jnp.exp(log_g * L)[None, :, None, None] * s_prev + jnp.einsum('blhd,blhe->bhde', kf * k_dec[None, :, :, None], vf)
    return s_new, o


def head_layernorm(o, g):
    mu = jnp.mean(o, axis=-1, keepdims=True)
    var = jnp.mean(jnp.square(o - mu), axis=-1, keepdims=True)
    y = (o - mu) * lax.rsqrt(var + EPS)
    return y.reshape(o.shape[0], o.shape[1], -1) * g.astype(jnp.float32)


def mem_kv(mem, mem_norm_g, w_ck, w_cv):
    m = rmsnorm(mem, mem_norm_g)
    return jnp.einsum('bmd,dhe->bmhe', m, w_ck), jnp.einsum('bmd,dhe->bmhe', m, w_cv)


def mem_attend(h, mk, mv, w_cq, w_co):
    b, t, _ = h.shape
    q = jnp.einsum('btd,dhe->bthe', h, w_cq)
    s = jnp.einsum('bthe,bmhe->bhtm', q, mk).astype(jnp.float32) * (MEM_DH ** -0.5)
    p = jax.nn.softmax(s, axis=-1).astype(mv.dtype)
    o = jnp.einsum('bhtm,bmhe->bthe', p, mv).reshape(b, t, MEM_HEADS * MEM_DH)
    return jnp.einsum('btc,cd->btd', o, w_co)


def conv_ffn(h, buf, w_up, conv_w, conv_b, w_down):
    t = h.shape[1]
    u = jnp.einsum('btd,df->btf', h, w_up)
    ext = jnp.concatenate([buf.astype(u.dtype), u], axis=1)
    c = conv_b + conv_w[0] * ext[:, 0:t]
    for j in range(1, CONV_W):
        c = c + conv_w[j] * ext[:, j:j + t]
    a, g = jnp.split(c, 2, axis=-1)
    y = jnp.einsum('btf,fd->btd', jax.nn.silu(a) * g, w_down)
    return y, ext[:, -(CONV_W - 1):]


def trunk_layer(x, pos, past_ckv, past_kr, ret_s0, mk, mv, conv_buf, w, is_prompt):
    b, t, _ = x.shape
    dt = x.dtype
    h = rmsnorm(x, w['norm_mix_g'])
    z = jnp.einsum('btd,dc->btc', h, w['w_in'])
    mf = mla_freqs()
    cq = rmsnorm(z[..., :O_KV], w['q_norm_g'])
    q = jnp.einsum('btr,rhd->bthd', cq, w['w_uq'])
    q = jnp.concatenate([q[..., :MLA_NOPE], rope(q[..., MLA_NOPE:], pos, mf)], -1)
    ckv = rmsnorm(z[..., O_KV:O_KR], w['kv_norm_g'])
    kr = rope(z[..., O_KR:O_RQ][:, :, None, :], pos, mf)[:, :, 0, :]
    if past_ckv is None:
        ckv_all, kr_all, pos_k = ckv, kr, pos
    else:
        ckv_all = jnp.concatenate([past_ckv.astype(dt), ckv], axis=1)
        kr_all = jnp.concatenate([past_kr.astype(dt), kr], axis=1)
        pos_k = jnp.arange(ckv_all.shape[1], dtype=jnp.int32)
    k_nope = jnp.einsum('bnr,rhd->bnhd', ckv_all, w['w_uk'])
    k = jnp.concatenate([k_nope, jnp.broadcast_to(kr_all[:, :, None, :], k_nope.shape[:3] + (MLA_ROPE,))], -1)
    v = jnp.einsum('bnr,rhd->bnhd', ckv_all, w['w_uv'])
    if is_prompt:
        nb = t // Q_BLOCK
        qb = q.reshape(b, nb, Q_BLOCK, MLA_HEADS, MLA_QK).transpose(1, 0, 2, 3, 4)
        pb = pos.reshape(nb, Q_BLOCK)
        ob = lax.map(lambda a: mla_attend(a[0], a[1], k, v, pos_k), (qb, pb))
        o_mla = ob.transpose(1, 0, 2, 3, 4).reshape(b, t, MLA_OUT)
    else:
        o_mla = mla_attend(q, pos, k, v, pos_k).reshape(b, t, MLA_OUT)
    rf = ret_freqs()
    rq = rope(z[..., O_RQ:O_RK].reshape(b, t, RET_HEADS, RET_DK), pos, rf)
    rk = rope(z[..., O_RK:O_RV].reshape(b, t, RET_HEADS, RET_DK), pos, rf) * (RET_DK ** -0.5)
    rv = z[..., O_RV:O_RG].reshape(b, t, RET_HEADS, RET_DV)
    log_g = retention_log_decay()
    s0 = ret_s0.astype(jnp.float32)
    if is_prompt:
        nc = t // CHUNK
        to_chunks = lambda a: a.reshape(b, nc, CHUNK, a.shape[2], a.shape[3]).transpose(1, 0, 2, 3, 4)
        s_fin, oc = lax.scan(lambda s, xs: retention_chunk(s, xs[0], xs[1], xs[2], log_g), s0,
                             (to_chunks(rq), to_chunks(rk), to_chunks(rv)))
        o_ret = oc.transpose(1, 0, 2, 3, 4).reshape(b, t, RET_HEADS, RET_DV)
    else:
        s_fin, o_ret = retention_chunk(s0, rq, rk, rv, log_g)
    o_ret = head_layernorm(o_ret, w['ret_gn_g']) * jax.nn.silu(z[..., O_RG:].astype(jnp.float32))
    mixed = jnp.concatenate([o_mla, o_ret.astype(dt)], axis=-1)
    x = x + jnp.einsum('btc,cd->btd', mixed, w['w_o'])
    x = x + mem_attend(rmsnorm(x, w['norm_mem_g']), mk, mv, w['w_cq'], w['w_co'])
    y, new_buf = conv_ffn(rmsnorm(x, w['norm_ffn_g']), conv_buf, w['w_up'], w['conv_w'], w['conv_b'], w['w_down'])
    x = x + y
    return x, ckv, kr, s_fin.astype(dt), new_buf


def setup_inputs(seed: int = 0) -> dict:
    key = jax.random.key(seed)
    ks = jax.random.split(key, 40)
    ctr = iter(range(40))
    f32 = jnp.float32
    L = DEPTH

    def nrm(shape, scale):
        return jax.random.normal(ks[next(ctr)], shape, f32) * scale

    def gain(shape):
        return 1.0 + nrm(shape, 0.02)

    return {
        'x_prompt': nrm((BATCH, SEQ, D_MODEL), 1.0),
        'x_sample': nrm((DEC_BATCH, DEC_SEQ, D_MODEL), 1.0),
        'cache_mla_ckv': nrm((L, DEC_BATCH, PAST_LEN, MLA_KV_RANK), 1.0),
        'cache_mla_krope': nrm((L, DEC_BATCH, PAST_LEN, MLA_ROPE), 1.0),
        'state_ret': nrm((L, DEC_BATCH, RET_HEADS, RET_DK, RET_DV), 0.3),
        'state_ffn_conv': nrm((L, DEC_BATCH, CONV_W - 1, 2 * D_FF), 1.0),
        'cache_mem_k': nrm((L, DEC_BATCH, MEM_TOKENS, MEM_HEADS, MEM_DH), 1.0),
        'cache_mem_v': nrm((L, DEC_BATCH, MEM_TOKENS, MEM_HEADS, MEM_DH), 1.0),
        'mem_prompt': nrm((BATCH, MEM_TOKENS, D_MODEL), 1.0),
        'norm_mix_g': gain((L, D_MODEL)),
        'w_in': nrm((L, D_MODEL, IN_COLS), D_MODEL ** -0.5),
        'q_norm_g': gain((L, MLA_Q_RANK)),
        'kv_norm_g': gain((L, MLA_KV_RANK)),
        'w_uq': nrm((L, MLA_Q_RANK, MLA_HEADS, MLA_QK), MLA_Q_RANK ** -0.5),
        'w_uk': nrm((L, MLA_KV_RANK, MLA_HEADS, MLA_NOPE), MLA_KV_RANK ** -0.5),
        'w_uv': nrm((L, MLA_KV_RANK, MLA_HEADS, MLA_V), MLA_KV_RANK ** -0.5),
        'ret_gn_g': gain((L, RET_OUT)),
        'w_o': nrm((L, D_MIX, D_MODEL), D_MIX ** -0.5),
        'norm_mem_g': gain((L, D_MODEL)),
        'mem_norm_g': gain((L, D_MODEL)),
        'w_cq': nrm((L, D_MODEL, MEM_HEADS, MEM_DH), D_MODEL ** -0.5),
        'w_ck': nrm((L, D_MODEL, MEM_HEADS, MEM_DH), D_MODEL ** -0.5),
        'w_cv': nrm((L, D_MODEL, MEM_HEADS, MEM_DH), D_MODEL ** -0.5),
        'w_co': nrm((L, MEM_HEADS * MEM_DH, D_MODEL), (MEM_HEADS * MEM_DH) ** -0.5),
        'norm_ffn_g': gain((L, D_MODEL)),
        'w_up': nrm((L, D_MODEL, 2 * D_FF), D_MODEL ** -0.5),
        'conv_w': nrm((L, CONV_W, 2 * D_FF), CONV_W ** -0.5),
        'conv_b': nrm((L, 2 * D_FF), 0.02),
        'w_down': nrm((L, D_FF, D_MODEL), D_FF ** -0.5),
        'final_norm_g': gain((D_MODEL,)),
    }


def reference(x_prompt, x_sample, cache_mla_ckv, cache_mla_krope, state_ret, state_ffn_conv,
              cache_mem_k, cache_mem_v, mem_prompt, norm_mix_g, w_in, q_norm_g, kv_norm_g,
              w_uq, w_uk, w_uv, ret_gn_g, w_o, norm_mem_g, mem_norm_g, w_cq, w_ck, w_cv, w_co,
              norm_ffn_g, w_up, conv_w, conv_b, w_down, final_norm_g):
    bp, tp, _ = x_prompt.shape
    ts = x_sample.shape[1]
    past = cache_mla_ckv.shape[2]
    pos_p = jnp.arange(tp, dtype=jnp.int32)
    pos_s = past + jnp.arange(ts, dtype=jnp.int32)
    hp, hs = x_prompt, x_sample
    p_ckv, p_kr, p_ret, p_conv, p_mk, p_mv = [], [], [], [], [], []
    s_ckv, s_kr, s_ret, s_conv = [], [], [], []
    for l in range(DEPTH):
        w = {'norm_mix_g': norm_mix_g[l], 'w_in': w_in[l], 'q_norm_g': q_norm_g[l],
             'kv_norm_g': kv_norm_g[l], 'w_uq': w_uq[l], 'w_uk': w_uk[l], 'w_uv': w_uv[l],
             'ret_gn_g': ret_gn_g[l], 'w_o': w_o[l], 'norm_mem_g': norm_mem_g[l],
             'w_cq': w_cq[l], 'w_co': w_co[l], 'norm_ffn_g': norm_ffn_g[l], 'w_up': w_up[l],
             'conv_w': conv_w[l], 'conv_b': conv_b[l], 'w_down': w_down[l]}
        mk_p, mv_p = mem_kv(mem_prompt, mem_norm_g[l], w_ck[l], w_cv[l])
        s0_p = jnp.zeros((bp, RET_HEADS, RET_DK, RET_DV), jnp.float32)
        buf0_p = jnp.zeros((bp, CONV_W - 1, 2 * D_FF), x_prompt.dtype)
        hp, ckv1, kr1, ret1, conv1 = trunk_layer(hp, pos_p, None, None, s0_p, mk_p, mv_p, buf0_p, w, True)
        hs, ckv2, kr2, ret2, conv2 = trunk_layer(hs, pos_s, cache_mla_ckv[l], cache_mla_krope[l], state_ret[l],
                                                 cache_mem_k[l], cache_mem_v[l], state_ffn_conv[l], w, False)
        p_ckv.append(ckv1); p_kr.append(kr1); p_ret.append(ret1); p_conv.append(conv1)
        p_mk.append(mk_p); p_mv.append(mv_p)
        s_ckv.append(ckv2); s_kr.append(kr2); s_ret.append(ret2); s_conv.append(conv2)
    y_prompt = rmsnorm(hp, final_norm_g)
    y_sample = rmsnorm(hs, final_norm_g)
    return (y_prompt, y_sample,
            jnp.stack(p_ckv), jnp.stack(p_kr), jnp.stack(p_ret), jnp.stack(p_conv),
            jnp.stack(p_mk), jnp.stack(p_mv),
            jnp.stack(s_ckv), jnp.stack(s_kr), jnp.stack(s_ret), jnp.stack(s_conv))
```

```python
import functools
import math

import jax
import jax.numpy as jnp
from jax import lax
from jax.experimental import pallas as pl
from jax.experimental.pallas import tpu as pltpu

F32 = jnp.float32
BF16 = jnp.bfloat16

EPS = 1e-6
ROPE_BASE = 10000.0
MLA_CHUNK = 64
LOG2E = 1.4426950408889634
LANES = 128
NEG_BIG = -1e30

ROW_TILE = 512
ATTN_TILE = 512
RET_BLOCK = 512
RET_CHUNK = 256
FF_CHUNK = 256
DEC_MEM_GROUP = 8
VMEM_LIMIT = 56 * 1024 * 1024


def _nt_dot(a, b):
    return lax.dot_general(a, b, (((1,), (1,)), ((), ())), preferred_element_type=F32)


def _dot(a, b):
    return jnp.dot(a, b, preferred_element_type=F32)


def _rms(x, g):
    return x * lax.rsqrt(jnp.mean(x * x, axis=-1, keepdims=True) + EPS) * g


def _silu(x):
    return x / (1.0 + jnp.exp(-x))


def _params(sem):
    return pltpu.CompilerParams(dimension_semantics=sem, vmem_limit_bytes=VMEM_LIMIT)


def _inproj_kernel(x_ref, g_ref, w1_ref, qg_ref, kvg_ref, wqq_ref, wk_ref, wv_ref,
                   cq_ref, sq_ref, ckr_ref, skr_ref, cr_ref, sr_ref,
                   q_out, k_out, v_out, ckv_out, kr_out, rq_out, rk_out, rv_out, rg_out,
                   *, cols, n_mla, n_ret, rope, rk_scale):
    o_q, o_kv, o_kr, o_krr, o_rq, o_rk, o_rv, o_rg, o_end = cols
    h = _rms(x_ref[...], g_ref[...]).astype(BF16)

    cq = _rms(_dot(h, w1_ref[:, o_q:o_kv]), qg_ref[...]).astype(BF16)
    qq = _dot(cq, wqq_ref[...])
    cq_t, sq_t = cq_ref[...], sq_ref[...]
    for hh in range(n_mla):
        a = qq[:, hh * LANES:(hh + 1) * LANES]
        b = qq[:, (n_mla + hh) * LANES:(n_mla + hh + 1) * LANES]
        q_out[:, hh * LANES:(hh + 1) * LANES] = (a * cq_t + b * sq_t).astype(BF16)

    ckv = _rms(_dot(h, w1_ref[:, o_kv:o_kr]), kvg_ref[...])
    ckv_out[...] = ckv
    kr = (_dot(h, w1_ref[:, o_kr:o_krr]) * ckr_ref[...]
          + _dot(h, w1_ref[:, o_krr:o_rq]) * skr_ref[...])
    kr_out[...] = kr[:, :rope]
    ckv_b = ckv.astype(BF16)
    k_out[...] = _dot(jnp.concatenate([ckv_b, kr.astype(BF16)], axis=-1), wk_ref[...]).astype(BF16)
    v_out[...] = _dot(ckv_b, wv_ref[...]).astype(BF16)

    cr_t, sr_t = cr_ref[...], sr_ref[...]
    zq = _dot(h, w1_ref[:, o_rq:o_rk])
    zk = _dot(h, w1_ref[:, o_rk:o_rv])
    for hh in range(n_ret):
        sl = slice(hh * LANES, (hh + 1) * LANES)
        a = zq[:, sl]
        rq_out[:, sl] = (a * cr_t + pltpu.roll(a, LANES // 2, 1) * sr_t).astype(BF16)
        b = zk[:, sl]
        rk_out[:, sl] = ((b * cr_t + pltpu.roll(b, LANES // 2, 1) * sr_t) * rk_scale).astype(BF16)
    rv_out[...] = _dot(h, w1_ref[:, o_rv:o_rg]).astype(BF16)
    rg_out[...] = _silu(_dot(h, w1_ref[:, o_rg:o_end])).astype(BF16)


def _inproj(x2d, g, w1, qg, kvg, wqq, wk, wv, tabs, *, cols, n_mla, n_ret, rope, kv_rank, v_dim):
    rows, d = x2d.shape
    tm = min(ROW_TILE, rows)
    assert rows % tm == 0
    tab_tiles = tabs[0].shape[0] // tm
    assert tabs[0].shape[0] % tm == 0
    ret_w = n_ret * LANES
    row = lambda w: pl.BlockSpec((tm, w), lambda i: (i, 0))
    full = lambda a: pl.BlockSpec(a.shape, lambda i: (0,) * a.ndim)
    tab = pl.BlockSpec((tm, LANES), lambda i: (i % tab_tiles, 0))
    out_shape = (
        jax.ShapeDtypeStruct((rows, n_mla * LANES), BF16),
        jax.ShapeDtypeStruct((rows, n_mla * LANES), BF16),
        jax.ShapeDtypeStruct((rows, n_mla * v_dim), BF16),
        jax.ShapeDtypeStruct((rows, kv_rank), F32),
        jax.ShapeDtypeStruct((rows, rope), F32),
        jax.ShapeDtypeStruct((rows, ret_w), BF16),
        jax.ShapeDtypeStruct((rows, ret_w), BF16),
        jax.ShapeDtypeStruct((rows, ret_w), BF16),
        jax.ShapeDtypeStruct((rows, ret_w), BF16),
    )
    kern = functools.partial(_inproj_kernel, cols=cols, n_mla=n_mla, n_ret=n_ret, rope=rope,
                             rk_scale=float(LANES) ** -0.5)
    return pl.pallas_call(
        kern,
        grid=(rows // tm,),
        in_specs=[row(d), full(g), full(w1), full(qg), full(kvg), full(wqq), full(wk), full(wv)] + [tab] * 6,
        out_specs=tuple(row(s.shape[1]) for s in out_shape),
        out_shape=out_shape,
        compiler_params=_params(("parallel",)),
        name="inproj",
    )(x2d, g, w1, qg, kvg, wqq, wk, wv, *tabs)


def _mla_attn_kernel(q_ref, k_ref, v_ref, o_ref, m_sc, l_sc, a_sc, *, tile, v_dim):
    qi = pl.program_id(2)
    m_sc[...] = jnp.full(m_sc.shape, -jnp.inf, F32)
    l_sc[...] = jnp.zeros(l_sc.shape, F32)
    a_sc[...] = jnp.zeros(a_sc.shape, F32)

    def step(kt, masked):
        ks = pl.multiple_of(kt * tile, tile)
        vblk = v_ref[0, pl.ds(ks, tile), :]
        for hh in range(2):
            q = q_ref[0, :, hh * LANES:(hh + 1) * LANES]
            k = k_ref[0, pl.ds(ks, tile), hh * LANES:(hh + 1) * LANES]
            s = _nt_dot(q, k)
            if masked:
                qc = lax.broadcasted_iota(jnp.int32, s.shape, 0) // MLA_CHUNK
                kc = lax.broadcasted_iota(jnp.int32, s.shape, 1) // MLA_CHUNK
                s = jnp.where(kc <= qc, s, NEG_BIG)
            m_old = m_sc[hh]
            m_new = jnp.maximum(m_old, jnp.max(s, axis=-1, keepdims=True))
            alpha = jnp.exp2(m_old - m_new)
            p = jnp.exp2(s - m_new)
            l_sc[hh] = alpha * l_sc[hh] + jnp.sum(p, axis=-1, keepdims=True)
            a_sc[hh] = alpha * a_sc[hh] + _dot(p.astype(BF16), vblk)
            m_sc[hh] = m_new

    def body(kt, c):
        step(kt, False)
        return c

    lax.fori_loop(0, qi, body, 0)
    step(qi, True)
    lane = lax.broadcasted_iota(jnp.int32, (tile, LANES), 1)
    o = jnp.where(lane < v_dim, a_sc[0] / l_sc[0], a_sc[1] / l_sc[1])
    o_ref[0] = o.astype(BF16)


def _mla_attn(q, k, v, *, v_dim):
    b, t, hw = q.shape
    n_pair = hw // (2 * LANES)
    assert 2 * v_dim == LANES and v.shape[2] == n_pair * LANES
    tile = min(ATTN_TILE, t)
    assert t % tile == 0 and tile % MLA_CHUNK == 0
    kern = functools.partial(_mla_attn_kernel, tile=tile, v_dim=v_dim)
    return pl.pallas_call(
        kern,
        grid=(b, n_pair, t // tile),
        in_specs=[
            pl.BlockSpec((1, tile, 2 * LANES), lambda bi, j, i: (bi, i, j)),
            pl.BlockSpec((1, t, 2 * LANES), lambda bi, j, i: (bi, 0, j)),
            pl.BlockSpec((1, t, LANES), lambda bi, j, i: (bi, 0, j)),
        ],
        out_specs=pl.BlockSpec((1, tile, LANES), lambda bi, j, i: (bi, i, j)),
        out_shape=jax.ShapeDtypeStruct((b, t, n_pair * LANES), BF16),
        scratch_shapes=[
            pltpu.VMEM((2, tile, 1), F32),
            pltpu.VMEM((2, tile, 1), F32),
            pltpu.VMEM((2, tile, LANES), F32),
        ],
        compiler_params=_params(("parallel", "parallel", "arbitrary")),
        name="mla_prompt_attn",
    )(q, k, v)


def _dec_attn_kernel(q_ref, cc_ref, kc_ref, cn_ref, kn_ref, wuk_ref, wuv_ref, o_ref, qa_sc, qr_sc,
                     *, n_heads, ts, past, nope, rope, v_dim):
    for hh in range(n_heads):
        qh = q_ref[0, :, hh * LANES:(hh + 1) * LANES]
        qa_sc[hh * ts:(hh + 1) * ts, :] = _dot(qh[:, :nope], wuk_ref[hh]).astype(BF16)
        qr_sc[hh * ts:(hh + 1) * ts, :] = qh[:, nope:nope + rope]
    qa, qr = qa_sc[...], qr_sc[...]
    cc = cc_ref[0].astype(BF16)
    cn = cn_ref[0].astype(BF16)
    s1 = _nt_dot(qa, cc) + _nt_dot(qr, kc_ref[0].astype(BF16))
    s2 = _nt_dot(qa, cn) + _nt_dot(qr, kn_ref[0].astype(BF16))
    rows = n_heads * ts
    qchunk1 = (past + lax.broadcasted_iota(jnp.int32, (rows, past), 0) % ts) // MLA_CHUNK
    kchunk1 = lax.broadcasted_iota(jnp.int32, (rows, past), 1) // MLA_CHUNK
    s1 = jnp.where(kchunk1 <= qchunk1, s1, NEG_BIG)
    qchunk2 = (past + lax.broadcasted_iota(jnp.int32, (rows, ts), 0) % ts) // MLA_CHUNK
    kchunk2 = (past + lax.broadcasted_iota(jnp.int32, (rows, ts), 1)) // MLA_CHUNK
    s2 = jnp.where(kchunk2 <= qchunk2, s2, NEG_BIG)
    m = jnp.maximum(jnp.max(s1, axis=-1, keepdims=True), jnp.max(s2, axis=-1, keepdims=True))
    p1 = jnp.exp2(s1 - m)
    p2 = jnp.exp2(s2 - m)
    l = jnp.sum(p1, axis=-1, keepdims=True) + jnp.sum(p2, axis=-1, keepdims=True)
    lat = ((_dot(p1.astype(BF16), cc) + _dot(p2.astype(BF16), cn)) / l).astype(BF16)
    per_slab = LANES // v_dim
    for j in range(n_heads // per_slab):
        parts = [_dot(lat[(j * per_slab + i) * ts:(j * per_slab + i + 1) * ts, :], wuv_ref[j * per_slab + i])
                 for i in range(per_slab)]
        o_ref[0, :, j * LANES:(j + 1) * LANES] = jnp.concatenate(parts, axis=-1).astype(BF16)


def _dec_attn(q, ckv_cache, kr_cache, ckv_new, kr_new, wuk_t, wuv_h, *, nope, rope, v_dim):
    b, ts, hw = q.shape
    n_heads = hw // LANES
    past, kv_rank = ckv_cache.shape[1:]
    kern = functools.partial(_dec_attn_kernel, n_heads=n_heads, ts=ts, past=past, nope=nope, rope=rope,
                             v_dim=v_dim)
    per_b = lambda a: pl.BlockSpec((1,) + a.shape[1:], lambda bi: (bi,) + (0,) * (a.ndim - 1))
    full = lambda a: pl.BlockSpec(a.shape, lambda bi: (0,) * a.ndim)
    return pl.pallas_call(
        kern,
        grid=(b,),
        in_specs=[per_b(q), per_b(ckv_cache), per_b(kr_cache), per_b(ckv_new), per_b(kr_new),
                  full(wuk_t), full(wuv_h)],
        out_specs=pl.BlockSpec((1, ts, n_heads * v_dim), lambda bi: (bi, 0, 0)),
        out_shape=jax.ShapeDtypeStruct((b, ts, n_heads * v_dim), BF16),
        scratch_shapes=[pltpu.VMEM((n_heads * ts, kv_rank), BF16), pltpu.VMEM((n_heads * ts, rope), BF16)],
        compiler_params=_params(("parallel",)),
        name="mla_decode_attn",
    )(q, ckv_cache, kr_cache, ckv_new, kr_new, wuk_t, wuv_h)


def _ret_kernel(rq_ref, rk_ref, rv_ref, gt_ref, s0_ref, g_ref, o_ref, sfin_ref, s_sc,
                *, lc, n_sub, log_g):
    t = pl.program_id(1)

    @pl.when(t == 0)
    def _():
        s_sc[...] = s0_ref[0]

    diff = (lax.broadcasted_iota(jnp.int32, (lc, lc), 0)
            - lax.broadcasted_iota(jnp.int32, (lc, lc), 1)).astype(F32)
    pos = lax.broadcasted_iota(jnp.int32, (lc, 1), 0).astype(F32)
    for hh, lg in enumerate(log_g):
        sl = slice(hh * LANES, (hh + 1) * LANES)
        dmask = jnp.where(diff >= 0, jnp.exp(lg * jnp.maximum(diff, 0.0)), 0.0)
        q_dec = jnp.exp(lg * (pos + 1.0))
        k_dec = jnp.exp(lg * (lc - 1.0 - pos))
        s_dec = math.exp(lg * lc)
        gain = g_ref[:, sl]
        for c in range(n_sub):
            rows = slice(c * lc, (c + 1) * lc)
            q = rq_ref[0, rows, sl]
            k = rk_ref[0, rows, sl]
            v = rv_ref[0, rows, sl]
            s_prev = s_sc[hh]
            inner = _nt_dot(q, k) * dmask
            o = _dot(inner.astype(BF16), v)
            o = o + _dot((q.astype(F32) * q_dec).astype(BF16), s_prev.astype(BF16))
            kd = (k.astype(F32) * k_dec).astype(BF16)
            s_sc[hh] = s_dec * s_prev + lax.dot_general(kd, v, (((0,), (0,)), ((), ())),
                                                        preferred_element_type=F32)
            mu = jnp.mean(o, axis=-1, keepdims=True)
            oc = o - mu
            var = jnp.mean(oc * oc, axis=-1, keepdims=True)
            y = oc * lax.rsqrt(var + EPS) * gain * gt_ref[0, rows, sl].astype(F32)
            o_ref[0, rows, sl] = y.astype(BF16)

    @pl.when(t == pl.num_programs(1) - 1)
    def _():
        sfin_ref[0] = s_sc[...]


def _retention(rq, rk, rv, gate, s0, gn_g, log_g):
    b, t, w = rq.shape
    n_heads, dk, dv = s0.shape[1:]
    lb = min(RET_BLOCK, t)
    lc = min(RET_CHUNK, lb)
    assert t % lb == 0 and lb % lc == 0 and dk == LANES and dv == LANES
    kern = functools.partial(_ret_kernel, lc=lc, n_sub=lb // lc, log_g=log_g)
    blk = pl.BlockSpec((1, lb, w), lambda bi, ti: (bi, ti, 0))
    st = pl.BlockSpec((1, n_heads, dk, dv), lambda bi, ti: (bi, 0, 0, 0))
    return pl.pallas_call(
        kern,
        grid=(b, t // lb),
        in_specs=[blk, blk, blk, blk, st, pl.BlockSpec(gn_g.shape, lambda bi, ti: (0, 0))],
        out_specs=(blk, st),
        out_shape=(jax.ShapeDtypeStruct((b, t, w), BF16), jax.ShapeDtypeStruct(s0.shape, F32)),
        scratch_shapes=[pltpu.VMEM((n_heads, dk, dv), F32)],
        compiler_params=_params(("parallel", "arbitrary")),
        name="retention",
    )(rq, rk, rv, gate, s0, gn_g)


def _memkv_kernel(m_ref, g_ref, w_ref, k_out, v_out):
    h = _rms(m_ref[...], g_ref[...]).astype(BF16)
    n = k_out.shape[1]
    k_out[...] = _dot(h, w_ref[:, :n])
    v_out[...] = _dot(h, w_ref[:, n:])


def _memkv(mem2d, g, wkv):
    rows, d = mem2d.shape
    n = wkv.shape[1] // 2
    tm = min(ROW_TILE, rows)
    assert rows % tm == 0
    row = lambda w: pl.BlockSpec((tm, w), lambda i: (i, 0))
    full = lambda a: pl.BlockSpec(a.shape, lambda i: (0,) * a.ndim)
    return pl.pallas_call(
        _memkv_kernel,
        grid=(rows // tm,),
        in_specs=[row(d), full(g), full(wkv)],
        out_specs=(row(n), row(n)),
        out_shape=(jax.ShapeDtypeStruct((rows, n), F32), jax.ShapeDtypeStruct((rows, n), F32)),
        compiler_params=_params(("parallel",)),
        name="mem_kv",
    )(mem2d, g, wkv)


def _mix_kernel(x_ref, om_ref, or_ref, wo_ref, g_ref, wcq_ref, mk_ref, mv_ref, wco_ref, o_ref,
                qc_sc, oc_sc, *, bb, tt, n_heads, dh):
    mixed = jnp.concatenate([om_ref[...], or_ref[...]], axis=-1)
    x1 = x_ref[...] + _dot(mixed, wo_ref[...])
    h = _rms(x1, g_ref[...]).astype(BF16)
    qc_sc[...] = (_dot(h, wcq_ref[...]) * (float(dh) ** -0.5)).astype(BF16)
    for bi in range(bb):
        rows = slice(bi * tt, (bi + 1) * tt)
        for hh in range(n_heads):
            sl = slice(hh * dh, (hh + 1) * dh)
            s = _nt_dot(qc_sc[rows, sl], mk_ref[bi, :, sl].astype(BF16))
            p = jnp.exp(s - jnp.max(s, axis=-1, keepdims=True))
            l = jnp.sum(p, axis=-1, keepdims=True)
            o = _dot(p.astype(BF16), mv_ref[bi, :, sl].astype(BF16)) / l
            oc_sc[rows, sl] = o.astype(BF16)
    o_ref[...] = x1 + _dot(oc_sc[...], wco_ref[...])


def _mix(x2d, om, orr, wo, g, wcq, mk, mv, wco, *, bb, tt, n_heads):
    rows, d = x2d.shape
    b, m_tok, _ = mk.shape
    r = bb * tt
    n_t = rows // (b * tt)
    assert rows % r == 0 and b % bb == 0 and (bb == 1 or n_t == 1)
    row = lambda w: pl.BlockSpec((r, w), lambda gi, ti: (gi * n_t + ti, 0))
    full = lambda a: pl.BlockSpec(a.shape, lambda gi, ti: (0,) * a.ndim)
    mem = pl.BlockSpec((bb, m_tok, d), lambda gi, ti: (gi, 0, 0))
    kern = functools.partial(_mix_kernel, bb=bb, tt=tt, n_heads=n_heads, dh=d // n_heads)
    return pl.pallas_call(
        kern,
        grid=(b // bb, n_t),
        in_specs=[row(d), row(om.shape[1]), row(orr.shape[1]), full(wo), full(g), full(wcq), mem, mem,
                  full(wco)],
        out_specs=row(d),
        out_shape=jax.ShapeDtypeStruct((rows, d), F32),
        scratch_shapes=[pltpu.VMEM((r, d), BF16), pltpu.VMEM((r, d), BF16)],
        compiler_params=_params(("parallel", "arbitrary")),
        name="mix_out_mem_attn",
    )(x2d, om, orr, wo, g, wcq, mk, mv, wco)


def _ffn_kernel(x_ref, g_ref, wup_ref, cw_ref, cb_ref, wdn_ref, st_ref, gf_ref, y_ref, st_out, carry,
                *, bb, tt, ff, fc):
    t = pl.program_id(1)

    @pl.when(t == 0)
    def _():
        carry[...] = st_ref[...]

    r = bb * tt
    x = x_ref[...]
    h = _rms(x, g_ref[...]).astype(BF16)
    rid = lax.broadcasted_iota(jnp.int32, (r, 1), 0) % tt
    first = rid == 0
    second = rid == 1

    def conv(cs):
        u = _dot(h, wup_ref[:, cs])
        p0 = jnp.broadcast_to(carry[:, 0:1, cs], (bb, tt, fc)).reshape(r, fc)
        p1 = jnp.broadcast_to(carry[:, 1:2, cs], (bb, tt, fc)).reshape(r, fc)
        um1 = jnp.where(first, p1, pltpu.roll(u, 1, 0))
        um2 = jnp.where(first, p0, jnp.where(second, p1, pltpu.roll(u, 2, 0)))
        carry[:, :, cs] = u.reshape(bb, tt, fc)[:, tt - 2:, :]
        return cb_ref[:, cs] + cw_ref[0:1, cs] * um2 + cw_ref[1:2, cs] * um1 + cw_ref[2:3, cs] * u

    acc = jnp.zeros(x.shape, F32)
    for c in range(ff // fc):
        a = conv(slice(c * fc, (c + 1) * fc))
        gte = conv(slice(ff + c * fc, ff + (c + 1) * fc))
        acc = acc + _dot((_silu(a) * gte).astype(BF16), wdn_ref[c * fc:(c + 1) * fc, :])
    y_ref[...] = _rms(x + acc, gf_ref[...])
    st_out[...] = carry[...]


def _ffn(x2d, g, wup, cw, cb, wdn, state, gfin, *, bb, tt):
    rows, d = x2d.shape
    b, n_keep, ff2 = state.shape
    assert n_keep == 2 and tt >= 2
    ff = ff2 // 2
    fc = FF_CHUNK
    assert ff % fc == 0
    r = bb * tt
    n_t = rows // (b * tt)
    assert rows % r == 0 and b % bb == 0 and (bb == 1 or n_t == 1)
    row = pl.BlockSpec((r, d), lambda gi, ti: (gi * n_t + ti, 0))
    full = lambda a: pl.BlockSpec(a.shape, lambda gi, ti: (0,) * a.ndim)
    st = pl.BlockSpec((bb, n_keep, ff2), lambda gi, ti: (gi, 0, 0))
    kern = functools.partial(_ffn_kernel, bb=bb, tt=tt, ff=ff, fc=fc)
    return pl.pallas_call(
        kern,
        grid=(b // bb, n_t),
        in_specs=[row, full(g), full(wup), full(cw), full(cb), full(wdn), st, full(gfin)],
        out_specs=(row, st),
        out_shape=(jax.ShapeDtypeStruct((rows, d), F32), jax.ShapeDtypeStruct(state.shape, F32)),
        scratch_shapes=[pltpu.VMEM((bb, n_keep, ff2), F32)],
        compiler_params=_params(("parallel", "arbitrary")),
        name="conv_ffn",
    )(x2d, g, wup, cw, cb, wdn, state, gfin)


def _rope_tables(pos, rope, q_scale):
    posf = pos.astype(F32)[:, None]
    half = rope // 2
    mf = 1.0 / (ROPE_BASE ** (jnp.arange(0, rope, 2, dtype=F32) / rope))
    cm, sm = jnp.cos(posf * mf[None, :]), jnp.sin(posf * mf[None, :])
    n = pos.shape[0]
    nope = LANES - 2 * rope
    z = lambda w: jnp.zeros((n, w), F32)
    cq = q_scale * jnp.concatenate([jnp.ones((n, nope), F32), cm, cm, z(rope)], axis=1)
    sq = q_scale * jnp.concatenate([z(nope), sm, sm, z(rope)], axis=1)
    ckr = jnp.concatenate([cm, cm, z(LANES - rope)], axis=1)
    skr = jnp.concatenate([sm, sm, z(LANES - rope)], axis=1)
    rf = 1.0 / (ROPE_BASE ** jnp.linspace(0.0, 1.0, LANES // 2, dtype=F32))
    cr_h, sr_h = jnp.cos(posf * rf[None, :]), jnp.sin(posf * rf[None, :])
    cr = jnp.concatenate([cr_h, cr_h], axis=1)
    sr = jnp.concatenate([-sr_h, sr_h], axis=1)
    del half
    return cq, sq, ckr, skr, cr, sr


def _layer_weights(w_in, w_uq, w_uk, w_uv, q_rank, kv_rank, rope, n_ret):
    d = w_in.shape[0]
    n_mla, qk = w_uq.shape[1:]
    nope = qk - rope
    v_dim = w_uv.shape[2]
    half = rope // 2
    assert nope + 2 * rope == LANES and kv_rank == LANES and w_uk.shape[2] == nope
    ret_w = n_ret * LANES
    o_kv, o_kr = q_rank, q_rank + kv_rank
    o_rq = o_kr + rope
    assert w_in.shape[1] == o_rq + 4 * ret_w
    wkr = w_in[:, o_kr:o_rq]
    wkr_rot = jnp.concatenate([-wkr[:, half:], wkr[:, :half]], axis=1)
    pad = jnp.zeros((d, LANES - rope), w_in.dtype)
    w1 = jnp.concatenate([w_in[:, :o_kr], wkr, pad, wkr_rot, pad, w_in[:, o_rq:]], axis=1).astype(BF16)
    c0 = o_kr
    cols = (0, o_kv, c0, c0 + LANES, c0 + 2 * LANES, c0 + 2 * LANES + ret_w, c0 + 2 * LANES + 2 * ret_w,
            c0 + 2 * LANES + 3 * ret_w, c0 + 2 * LANES + 4 * ret_w)

    zq = lambda w: jnp.zeros((q_rank, n_mla, w), w_uq.dtype)
    r1, r2 = w_uq[..., nope:nope + half], w_uq[..., nope + half:]
    wq = jnp.concatenate([w_uq, zq(rope)], axis=-1).reshape(q_rank, n_mla * LANES)
    wq_rot = jnp.concatenate([zq(nope), -r2, r1, zq(rope)], axis=-1).reshape(q_rank, n_mla * LANES)
    wqq = jnp.concatenate([wq, wq_rot], axis=1).astype(BF16)

    wk_nope = jnp.concatenate([w_uk, jnp.zeros((kv_rank, n_mla, LANES - nope), w_uk.dtype)], axis=-1)
    place = jnp.concatenate([jnp.zeros((rope, nope), F32), jnp.eye(rope, dtype=F32),
                             jnp.zeros((rope, LANES - nope - rope), F32)], axis=1)
    wk_rope = jnp.concatenate([jnp.broadcast_to(place[:, None, :], (rope, n_mla, LANES)),
                               jnp.zeros((LANES - rope, n_mla, LANES), F32)], axis=0)
    wk = jnp.concatenate([wk_nope, wk_rope], axis=0).reshape(kv_rank + LANES, n_mla * LANES).astype(BF16)
    wv = w_uv.reshape(kv_rank, n_mla * v_dim).astype(BF16)
    wuk_t = jnp.transpose(w_uk, (1, 2, 0)).astype(BF16)
    wuv_h = jnp.transpose(w_uv, (1, 0, 2)).astype(BF16)
    return w1, cols, wqq, wk, wv, wuk_t, wuv_h, nope, v_dim, n_mla


def kernel(x_prompt, x_sample, cache_mla_ckv, cache_mla_krope, state_ret, state_ffn_conv, cache_mem_k,
           cache_mem_v, mem_prompt, norm_mix_g, w_in, q_norm_g, kv_norm_g, w_uq, w_uk, w_uv, ret_gn_g, w_o,
           norm_mem_g, mem_norm_g, w_cq, w_ck, w_cv, w_co, norm_ffn_g, w_up, conv_w, conv_b, w_down,
           final_norm_g):
    bp, tp, d = x_prompt.shape
    bs, ts, _ = x_sample.shape
    depth = w_in.shape[0]
    past = cache_mla_ckv.shape[2]
    q_rank, kv_rank = w_uq.shape[1], w_uk.shape[1]
    rope = cache_mla_krope.shape[3]
    n_ret, ret_dk, ret_dv = state_ret.shape[2:]
    mem_tok, mem_heads, mem_dh = cache_mem_k.shape[2:]
    ff2 = w_up.shape[2]
    assert conv_w.shape[1] == 3 and ret_dk == LANES
    qk = w_uq.shape[3]
    q_scale = float(qk) ** -0.5 * LOG2E
    log_g = tuple(math.log(1.0 - 2.0 ** (-5.0 - i)) for i in range(n_ret))

    tabs_p = _rope_tables(jnp.arange(tp, dtype=jnp.int32), rope, q_scale)
    tabs_s = _rope_tables(jnp.tile(past + jnp.arange(ts, dtype=jnp.int32), bs), rope, q_scale)

    hp = x_prompt.reshape(bp * tp, d)
    hs = x_sample.reshape(bs * ts, d)
    row2 = lambda a: a.reshape(1, -1)
    outs = {k: [] for k in ("p_ckv", "p_kr", "p_ret", "p_conv", "p_mk", "p_mv", "s_ckv", "s_kr", "s_ret", "s_conv")}
    tile_p = min(ROW_TILE, tp)
    for l in range(depth):
        w1, cols, wqq, wk, wv, wuk_t, wuv_h, nope, v_dim, n_mla = _layer_weights(
            w_in[l], w_uq[l], w_uk[l], w_uv[l], q_rank, kv_rank, rope, n_ret)
        inproj = functools.partial(
            _inproj, g=row2(norm_mix_g[l]), w1=w1, qg=row2(q_norm_g[l]), kvg=row2(kv_norm_g[l]), wqq=wqq,
            wk=wk, wv=wv, cols=cols, n_mla=n_mla, n_ret=n_ret, rope=rope, kv_rank=kv_rank, v_dim=v_dim)
        wo = w_o[l].astype(BF16)
        wcq = w_cq[l].reshape(d, mem_heads * mem_dh).astype(BF16)
        wco = w_co[l].astype(BF16)
        wckv = jnp.concatenate([w_ck[l].reshape(d, -1), w_cv[l].reshape(d, -1)], axis=1).astype(BF16)
        wup = w_up[l].astype(BF16)
        wdn = w_down[l].astype(BF16)
        gn = row2(ret_gn_g[l])
        mix = functools.partial(_mix, wo=wo, g=row2(norm_mem_g[l]), wcq=wcq, wco=wco, n_heads=mem_heads)
        ffn = functools.partial(_ffn, g=row2(norm_ffn_g[l]), wup=wup, cw=conv_w[l], cb=row2(conv_b[l]),
                                wdn=wdn, gfin=row2(final_norm_g))

        q, k, v, ckv, kr, rq, rk, rv, rg = inproj(hp, tabs=tabs_p)
        o_mla = _mla_attn(q.reshape(bp, tp, -1), k.reshape(bp, tp, -1), v.reshape(bp, tp, -1), v_dim=v_dim)
        b3 = lambda a: a.reshape(bp, tp, -1)
        o_ret, ret_fin = _retention(b3(rq), b3(rk), b3(rv), b3(rg),
                                    jnp.zeros((bp, n_ret, ret_dk, ret_dv), F32), gn, log_g)
        mk, mv = _memkv(mem_prompt.reshape(bp * mem_tok, d), row2(mem_norm_g[l]), wckv)
        x2 = mix(hp, o_mla.reshape(bp * tp, -1), o_ret.reshape(bp * tp, -1),
                 mk=mk.reshape(bp, mem_tok, -1), mv=mv.reshape(bp, mem_tok, -1), bb=1, tt=tile_p)
        hp_next, conv_fin = ffn(x2, state=jnp.zeros((bp, 2, ff2), F32), bb=1, tt=tile_p)
        outs["p_ckv"].append(ckv.reshape(bp, tp, kv_rank))
        outs["p_kr"].append(kr.reshape(bp, tp, rope))
        outs["p_ret"].append(ret_fin)
        outs["p_conv"].append(conv_fin)
        outs["p_mk"].append(mk.reshape(bp, mem_tok, mem_heads, mem_dh))
        outs["p_mv"].append(mv.reshape(bp, mem_tok, mem_heads, mem_dh))

        q, k, v, ckv, kr, rq, rk, rv, rg = inproj(hs, tabs=tabs_s)
        o_mla = _dec_attn(q.reshape(bs, ts, -1), cache_mla_ckv[l], cache_mla_krope[l],
                          ckv.reshape(bs, ts, -1), kr.reshape(bs, ts, -1), wuk_t, wuv_h,
                          nope=nope, rope=rope, v_dim=v_dim)
        s3 = lambda a: a.reshape(bs, ts, -1)
        o_ret, ret_fin = _retention(s3(rq), s3(rk), s3(rv), s3(rg), state_ret[l], gn, log_g)
        bb = math.gcd(bs, DEC_MEM_GROUP)
        x2 = mix(hs, o_mla.reshape(bs * ts, -1), o_ret.reshape(bs * ts, -1),
                 mk=cache_mem_k[l].reshape(bs, mem_tok, -1), mv=cache_mem_v[l].reshape(bs, mem_tok, -1),
                 bb=bb, tt=ts)
        hs_next, conv_fin = ffn(x2, state=state_ffn_conv[l], bb=bs, tt=ts)
        outs["s_ckv"].append(ckv.reshape(bs, ts, kv_rank))
        outs["s_kr"].append(kr.reshape(bs, ts, rope))
        outs["s_ret"].append(ret_fin)
        outs["s_conv"].append(conv_fin)
        assert depth == 1
        hp, hs = hp_next, hs_next

    st = lambda name: jnp.stack(outs[name])
    return (hp.reshape(bp, tp, d), hs.reshape(bs, ts, d),
            st("p_ckv"), st("p_kr"), st("p_ret"), st("p_conv"), st("p_mk"), st("p_mv"),
            st("s_ckv"), st("s_kr"), st("s_ret"), st("s_conv"))
```

```python
import functools
import math

import jax
import jax.numpy as jnp
from jax import lax
from jax.experimental import pallas as pl
from jax.experimental.pallas import tpu as pltpu

F32 = jnp.float32
BF16 = jnp.bfloat16

EPS = 1e-6
ROPE_BASE = 10000.0
MLA_CHUNK = 64
LOG2E = 1.4426950408889634
LANES = 128
NEG_BIG = -1e30

ROW_TILE = 512
ATTN_Q_TILE = 1024
ATTN_K_TILE = 512
RET_BLOCK = 512
RET_CHUNK = 256
FF_CHUNK = 256
DEC_MEM_GROUP = 8
VMEM_LIMIT = 56 * 1024 * 1024


def _nt_dot(a, b):
    return lax.dot_general(a, b, (((1,), (1,)), ((), ())), preferred_element_type=F32)


def _dot(a, b):
    return jnp.dot(a, b, preferred_element_type=F32)


def _rms(x, g):
    return x * lax.rsqrt(jnp.mean(x * x, axis=-1, keepdims=True) + EPS) * g


def _silu(x):
    return x / (1.0 + jnp.exp(-x))


def _params(sem):
    return pltpu.CompilerParams(dimension_semantics=sem, vmem_limit_bytes=VMEM_LIMIT)


def _inproj_kernel(x_ref, g_ref, w1_ref, qg_ref, kvg_ref, wqq_ref, wk_ref, wv_ref,
                   cq_ref, sq_ref, ckr_ref, skr_ref, cr_ref, sr_ref,
                   q_out, k_out, v_out, ckv_out, kr_out, rq_out, rk_out, rv_out, rg_out,
                   *, cols, n_mla, n_ret, rope, v_dim, rk_scale):
    o_q, o_kv, o_kr, o_krr, o_rq, o_rk, o_rv, o_rg, o_end = cols
    h = _rms(x_ref[...], g_ref[...]).astype(BF16)

    cq = _rms(_dot(h, w1_ref[:, o_q:o_kv]), qg_ref[...]).astype(BF16)
    qq = _dot(cq, wqq_ref[...])
    cq_t, sq_t = cq_ref[...], sq_ref[...]
    for hh in range(n_mla):
        a = qq[:, hh * LANES:(hh + 1) * LANES]
        b = qq[:, (n_mla + hh) * LANES:(n_mla + hh + 1) * LANES]
        q_out[:, hh * LANES:(hh + 1) * LANES] = (a * cq_t + b * sq_t).astype(BF16)

    ckv = _rms(_dot(h, w1_ref[:, o_kv:o_kr]), kvg_ref[...])
    ckv_out[...] = ckv
    kr = (_dot(h, w1_ref[:, o_kr:o_krr]) * ckr_ref[...]
          + _dot(h, w1_ref[:, o_krr:o_rq]) * skr_ref[...])
    kr_out[...] = kr[:, :rope]
    ckv_b = ckv.astype(BF16)
    k_out[...] = _dot(jnp.concatenate([ckv_b, kr.astype(BF16)], axis=-1), wk_ref[...]).astype(BF16)
    vlane = lax.broadcasted_iota(jnp.int32, (1, n_mla * LANES), 1) % (2 * LANES)
    ones_col = ((vlane == v_dim) | (vlane == LANES)).astype(F32)
    v_out[...] = (_dot(ckv_b, wv_ref[...]) + ones_col).astype(BF16)

    cr_t, sr_t = cr_ref[...], sr_ref[...]
    zq = _dot(h, w1_ref[:, o_rq:o_rk])
    zk = _dot(h, w1_ref[:, o_rk:o_rv])
    for hh in range(n_ret):
        sl = slice(hh * LANES, (hh + 1) * LANES)
        a = zq[:, sl]
        rq_out[:, sl] = (a * cr_t + pltpu.roll(a, LANES // 2, 1) * sr_t).astype(BF16)
        b = zk[:, sl]
        rk_out[:, sl] = ((b * cr_t + pltpu.roll(b, LANES // 2, 1) * sr_t) * rk_scale).astype(BF16)
    rv_out[...] = _dot(h, w1_ref[:, o_rv:o_rg]).astype(BF16)
    rg_out[...] = _silu(_dot(h, w1_ref[:, o_rg:o_end])).astype(BF16)


def _inproj(x2d, g, w1, qg, kvg, wqq, wk, wv, tabs, *, cols, n_mla, n_ret, rope, kv_rank, v_dim):
    rows, d = x2d.shape
    tm = min(ROW_TILE, rows)
    assert rows % tm == 0
    tab_tiles = tabs[0].shape[0] // tm
    assert tabs[0].shape[0] % tm == 0
    ret_w = n_ret * LANES
    row = lambda w: pl.BlockSpec((tm, w), lambda i: (i, 0))
    full = lambda a: pl.BlockSpec(a.shape, lambda i: (0,) * a.ndim)
    tab = pl.BlockSpec((tm, LANES), lambda i: (i % tab_tiles, 0))
    out_shape = (
        jax.ShapeDtypeStruct((rows, n_mla * LANES), BF16),
        jax.ShapeDtypeStruct((rows, n_mla * LANES), BF16),
        jax.ShapeDtypeStruct((rows, n_mla * LANES), BF16),
        jax.ShapeDtypeStruct((rows, kv_rank), F32),
        jax.ShapeDtypeStruct((rows, rope), F32),
        jax.ShapeDtypeStruct((rows, ret_w), BF16),
        jax.ShapeDtypeStruct((rows, ret_w), BF16),
        jax.ShapeDtypeStruct((rows, ret_w), BF16),
        jax.ShapeDtypeStruct((rows, ret_w), BF16),
    )
    kern = functools.partial(_inproj_kernel, cols=cols, n_mla=n_mla, n_ret=n_ret, rope=rope, v_dim=v_dim,
                             rk_scale=float(LANES) ** -0.5)
    return pl.pallas_call(
        kern,
        grid=(rows // tm,),
        in_specs=[row(d), full(g), full(w1), full(qg), full(kvg), full(wqq), full(wk), full(wv)] + [tab] * 6,
        out_specs=tuple(row(s.shape[1]) for s in out_shape),
        out_shape=out_shape,
        compiler_params=_params(("parallel",)),
        name="inproj",
    )(x2d, g, w1, qg, kvg, wqq, wk, wv, *tabs)


def _mla_attn_kernel(q_ref, k_ref, v_ref, o_ref, m_sc, a_sc, *, tq, tk, v_dim):
    qi = pl.program_id(2)
    m_sc[...] = jnp.full(m_sc.shape, -jnp.inf, F32)
    a_sc[...] = jnp.zeros(a_sc.shape, F32)
    n_rep = tk // LANES

    def step(kt, r0, r1, bias):
        ks = pl.multiple_of(kt * tk, tk)
        for hh in range(2):
            sl = slice(hh * LANES, (hh + 1) * LANES)
            s = _nt_dot(q_ref[0, r0:r1, sl], k_ref[0, pl.ds(ks, tk), sl])
            if bias is not None:
                s = s + bias
            m_old = m_sc[hh, r0:r1, :]
            m_new = jnp.maximum(m_old, jnp.max(s, axis=-1, keepdims=True))
            p = jnp.exp2(s - jnp.tile(m_new, (1, n_rep)))
            a_sc[hh, r0:r1, :] = (jnp.exp2(m_old - m_new) * a_sc[hh, r0:r1, :]
                                  + _dot(p.astype(BF16), v_ref[0, pl.ds(ks, tk), sl]))
            m_sc[hh, r0:r1, :] = m_new

    def body(kt, c):
        step(kt, 0, tq, None)
        return c

    n_diag = tq // tk
    n_full = qi * n_diag
    lax.fori_loop(0, n_full, body, 0)
    row_c = lax.broadcasted_iota(jnp.int32, (tk, tk), 0) // MLA_CHUNK
    col_c = lax.broadcasted_iota(jnp.int32, (tk, tk), 1) // MLA_CHUNK
    diag_bias = jnp.where(col_c <= row_c, 0.0, NEG_BIG)
    for i in range(n_diag):
        step(n_full + i, i * tk, (i + 1) * tk, diag_bias)
        if (i + 1) * tk < tq:
            step(n_full + i, (i + 1) * tk, tq, None)
    a0, a1 = a_sc[0], a_sc[1]
    lane = lax.broadcasted_iota(jnp.int32, (tq, LANES), 1)
    o = jnp.where(lane < v_dim, a0 / a0[:, v_dim:v_dim + 1], a1 / a1[:, 0:1])
    o_ref[0] = o.astype(BF16)


def _mla_attn(q, k, v, *, v_dim):
    b, t, hw = q.shape
    n_pair = hw // (2 * LANES)
    assert 2 * v_dim == LANES and v.shape[2] == hw
    tq = min(ATTN_Q_TILE, t)
    tk = min(ATTN_K_TILE, tq)
    assert t % tq == 0 and tq % tk == 0 and tk % MLA_CHUNK == 0
    kern = functools.partial(_mla_attn_kernel, tq=tq, tk=tk, v_dim=v_dim)
    resident = lambda: pl.BlockSpec((1, t, 2 * LANES), lambda bi, j, i: (bi, 0, j),
                                    pipeline_mode=pl.Buffered(1))
    return pl.pallas_call(
        kern,
        grid=(b, n_pair, t // tq),
        in_specs=[pl.BlockSpec((1, tq, 2 * LANES), lambda bi, j, i: (bi, i, j)), resident(), resident()],
        out_specs=pl.BlockSpec((1, tq, LANES), lambda bi, j, i: (bi, i, j)),
        out_shape=jax.ShapeDtypeStruct((b, t, n_pair * LANES), BF16),
        scratch_shapes=[pltpu.VMEM((2, tq, LANES), F32), pltpu.VMEM((2, tq, LANES), F32)],
        compiler_params=_params(("parallel", "parallel", "arbitrary")),
        name="mla_prompt_attn",
    )(q, k, v)


def _dec_attn_kernel(q_ref, cc_ref, kc_ref, cn_ref, kn_ref, wuk_ref, wuv_ref, o_ref, qa_sc, qr_sc,
                     *, n_heads, ts, past, nope, rope, v_dim):
    for hh in range(n_heads):
        qh = q_ref[0, :, hh * LANES:(hh + 1) * LANES]
        qa_sc[hh * ts:(hh + 1) * ts, :] = _dot(qh[:, :nope], wuk_ref[hh]).astype(BF16)
        qr_sc[hh * ts:(hh + 1) * ts, :] = qh[:, nope:nope + rope]
    qa, qr = qa_sc[...], qr_sc[...]
    cc = cc_ref[0].astype(BF16)
    cn = cn_ref[0].astype(BF16)
    s1 = _nt_dot(qa, cc) + _nt_dot(qr, kc_ref[0].astype(BF16))
    s2 = _nt_dot(qa, cn) + _nt_dot(qr, kn_ref[0].astype(BF16))
    rows = n_heads * ts
    qchunk1 = (past + lax.broadcasted_iota(jnp.int32, (rows, past), 0) % ts) // MLA_CHUNK
    kchunk1 = lax.broadcasted_iota(jnp.int32, (rows, past), 1) // MLA_CHUNK
    s1 = jnp.where(kchunk1 <= qchunk1, s1, NEG_BIG)
    qchunk2 = (past + lax.broadcasted_iota(jnp.int32, (rows, ts), 0) % ts) // MLA_CHUNK
    kchunk2 = (past + lax.broadcasted_iota(jnp.int32, (rows, ts), 1)) // MLA_CHUNK
    s2 = jnp.where(kchunk2 <= qchunk2, s2, NEG_BIG)
    m = jnp.maximum(jnp.max(s1, axis=-1, keepdims=True), jnp.max(s2, axis=-1, keepdims=True))
    p1 = jnp.exp2(s1 - m)
    p2 = jnp.exp2(s2 - m)
    l = jnp.sum(p1, axis=-1, keepdims=True) + jnp.sum(p2, axis=-1, keepdims=True)
    lat = ((_dot(p1.astype(BF16), cc) + _dot(p2.astype(BF16), cn)) / l).astype(BF16)
    per_slab = LANES // v_dim
    for j in range(n_heads // per_slab):
        parts = [_dot(lat[(j * per_slab + i) * ts:(j * per_slab + i + 1) * ts, :], wuv_ref[j * per_slab + i])
                 for i in range(per_slab)]
        o_ref[0, :, j * LANES:(j + 1) * LANES] = jnp.concatenate(parts, axis=-1).astype(BF16)


def _dec_attn(q, ckv_cache, kr_cache, ckv_new, kr_new, wuk_t, wuv_h, *, nope, rope, v_dim):
    b, ts, hw = q.shape
    n_heads = hw // LANES
    past, kv_rank = ckv_cache.shape[1:]
    kern = functools.partial(_dec_attn_kernel, n_heads=n_heads, ts=ts, past=past, nope=nope, rope=rope,
                             v_dim=v_dim)
    per_b = lambda a: pl.BlockSpec((1,) + a.shape[1:], lambda bi: (bi,) + (0,) * (a.ndim - 1))
    full = lambda a: pl.BlockSpec(a.shape, lambda bi: (0,) * a.ndim)
    return pl.pallas_call(
        kern,
        grid=(b,),
        in_specs=[per_b(q), per_b(ckv_cache), per_b(kr_cache), per_b(ckv_new), per_b(kr_new),
                  full(wuk_t), full(wuv_h)],
        out_specs=pl.BlockSpec((1, ts, n_heads * v_dim), lambda bi: (bi, 0, 0)),
        out_shape=jax.ShapeDtypeStruct((b, ts, n_heads * v_dim), BF16),
        scratch_shapes=[pltpu.VMEM((n_heads * ts, kv_rank), BF16), pltpu.VMEM((n_heads * ts, rope), BF16)],
        compiler_params=_params(("parallel",)),
        name="mla_decode_attn",
    )(q, ckv_cache, kr_cache, ckv_new, kr_new, wuk_t, wuv_h)


def _ret_kernel(rq_ref, rk_ref, rv_ref, gt_ref, s0_ref, g_ref, o_ref, sfin_ref, s_sc,
                *, lc, n_sub, log_g):
    t = pl.program_id(1)

    @pl.when(t == 0)
    def _():
        s_sc[...] = s0_ref[0]

    diff = (lax.broadcasted_iota(jnp.int32, (lc, lc), 0)
            - lax.broadcasted_iota(jnp.int32, (lc, lc), 1)).astype(F32)
    pos = lax.broadcasted_iota(jnp.int32, (lc, 1), 0).astype(F32)
    for hh, lg in enumerate(log_g):
        sl = slice(hh * LANES, (hh + 1) * LANES)
        dmask = jnp.where(diff >= 0, jnp.exp(lg * jnp.maximum(diff, 0.0)), 0.0)
        q_dec = jnp.exp(lg * (pos + 1.0))
        k_dec = jnp.exp(lg * (lc - 1.0 - pos))
        s_dec = math.exp(lg * lc)
        gain = g_ref[:, sl]
        for c in range(n_sub):
            rows = slice(c * lc, (c + 1) * lc)
            q = rq_ref[0, rows, sl]
            k = rk_ref[0, rows, sl]
            v = rv_ref[0, rows, sl]
            s_prev = s_sc[hh]
            inner = _nt_dot(q, k) * dmask
            o = _dot(inner.astype(BF16), v)
            o = o + _dot((q.astype(F32) * q_dec).astype(BF16), s_prev.astype(BF16))
            kd = (k.astype(F32) * k_dec).astype(BF16)
            s_sc[hh] = s_dec * s_prev + lax.dot_general(kd, v, (((0,), (0,)), ((), ())),
                                                        preferred_element_type=F32)
            mu = jnp.mean(o, axis=-1, keepdims=True)
            oc = o - mu
            var = jnp.mean(oc * oc, axis=-1, keepdims=True)
            y = oc * lax.rsqrt(var + EPS) * gain * gt_ref[0, rows, sl].astype(F32)
            o_ref[0, rows, sl] = y.astype(BF16)

    @pl.when(t == pl.num_programs(1) - 1)
    def _():
        sfin_ref[0] = s_sc[...]


def _retention(rq, rk, rv, gate, s0, gn_g, log_g):
    b, t, w = rq.shape
    n_heads, dk, dv = s0.shape[1:]
    lb = min(RET_BLOCK, t)
    lc = min(RET_CHUNK, lb)
    assert t % lb == 0 and lb % lc == 0 and dk == LANES and dv == LANES
    kern = functools.partial(_ret_kernel, lc=lc, n_sub=lb // lc, log_g=log_g)
    blk = pl.BlockSpec((1, lb, w), lambda bi, ti: (bi, ti, 0))
    st = pl.BlockSpec((1, n_heads, dk, dv), lambda bi, ti: (bi, 0, 0, 0))
    return pl.pallas_call(
        kern,
        grid=(b, t // lb),
        in_specs=[blk, blk, blk, blk, st, pl.BlockSpec(gn_g.shape, lambda bi, ti: (0, 0))],
        out_specs=(blk, st),
        out_shape=(jax.ShapeDtypeStruct((b, t, w), BF16), jax.ShapeDtypeStruct(s0.shape, F32)),
        scratch_shapes=[pltpu.VMEM((n_heads, dk, dv), F32)],
        compiler_params=_params(("parallel", "arbitrary")),
        name="retention",
    )(rq, rk, rv, gate, s0, gn_g)


def _memkv_kernel(m_ref, g_ref, w_ref, k_out, v_out):
    h = _rms(m_ref[...], g_ref[...]).astype(BF16)
    n = k_out.shape[1]
    k_out[...] = _dot(h, w_ref[:, :n])
    v_out[...] = _dot(h, w_ref[:, n:])


def _memkv(mem2d, g, wkv):
    rows, d = mem2d.shape
    n = wkv.shape[1] // 2
    tm = min(ROW_TILE, rows)
    assert rows % tm == 0
    row = lambda w: pl.BlockSpec((tm, w), lambda i: (i, 0))
    full = lambda a: pl.BlockSpec(a.shape, lambda i: (0,) * a.ndim)
    return pl.pallas_call(
        _memkv_kernel,
        grid=(rows // tm,),
        in_specs=[row(d), full(g), full(wkv)],
        out_specs=(row(n), row(n)),
        out_shape=(jax.ShapeDtypeStruct((rows, n), F32), jax.ShapeDtypeStruct((rows, n), F32)),
        compiler_params=_params(("parallel",)),
        name="mem_kv",
    )(mem2d, g, wkv)


def _mix_kernel(x_ref, om_ref, or_ref, wo_ref, g_ref, wcq_ref, mk_ref, mv_ref, wco_ref, o_ref,
                qc_sc, oc_sc, *, bb, tt, n_heads, dh):
    mixed = jnp.concatenate([om_ref[...], or_ref[...]], axis=-1)
    x1 = x_ref[...] + _dot(mixed, wo_ref[...])
    h = _rms(x1, g_ref[...]).astype(BF16)
    qc_sc[...] = (_dot(h, wcq_ref[...]) * (float(dh) ** -0.5)).astype(BF16)
    for bi in range(bb):
        rows = slice(bi * tt, (bi + 1) * tt)
        for hh in range(n_heads):
            sl = slice(hh * dh, (hh + 1) * dh)
            s = _nt_dot(qc_sc[rows, sl], mk_ref[bi, :, sl].astype(BF16))
            p = jnp.exp(s - jnp.max(s, axis=-1, keepdims=True))
            l = jnp.sum(p, axis=-1, keepdims=True)
            o = _dot(p.astype(BF16), mv_ref[bi, :, sl].astype(BF16)) / l
            oc_sc[rows, sl] = o.astype(BF16)
    o_ref[...] = x1 + _dot(oc_sc[...], wco_ref[...])


def _mix(x2d, om, orr, wo, g, wcq, mk, mv, wco, *, bb, tt, n_heads):
    rows, d = x2d.shape
    b, m_tok, _ = mk.shape
    r = bb * tt
    n_t = rows // (b * tt)
    assert rows % r == 0 and b % bb == 0 and (bb == 1 or n_t == 1)
    row = lambda w: pl.BlockSpec((r, w), lambda gi, ti: (gi * n_t + ti, 0))
    full = lambda a: pl.BlockSpec(a.shape, lambda gi, ti: (0,) * a.ndim)
    mem = pl.BlockSpec((bb, m_tok, d), lambda gi, ti: (gi, 0, 0))
    kern = functools.partial(_mix_kernel, bb=bb, tt=tt, n_heads=n_heads, dh=d // n_heads)
    return pl.pallas_call(
        kern,
        grid=(b // bb, n_t),
        in_specs=[row(d), row(om.shape[1]), row(orr.shape[1]), full(wo), full(g), full(wcq), mem, mem,
                  full(wco)],
        out_specs=row(d),
        out_shape=jax.ShapeDtypeStruct((rows, d), F32),
        scratch_shapes=[pltpu.VMEM((r, d), BF16), pltpu.VMEM((r, d), BF16)],
        compiler_params=_params(("parallel", "arbitrary")),
        name="mix_out_mem_attn",
    )(x2d, om, orr, wo, g, wcq, mk, mv, wco)


def _ffn_kernel(x_ref, g_ref, wup_ref, cw_ref, cb_ref, wdn_ref, st_ref, gf_ref, y_ref, st_out, carry,
                *, bb, tt, ff, fc):
    t = pl.program_id(1)

    @pl.when(t == 0)
    def _():
        carry[...] = st_ref[...]

    r = bb * tt
    x = x_ref[...]
    h = _rms(x, g_ref[...]).astype(BF16)
    rid = lax.broadcasted_iota(jnp.int32, (r, 1), 0) % tt
    first = rid == 0
    second = rid == 1

    def conv(cs):
        u = _dot(h, wup_ref[:, cs])
        p0 = jnp.broadcast_to(carry[:, 0:1, cs], (bb, tt, fc)).reshape(r, fc)
        p1 = jnp.broadcast_to(carry[:, 1:2, cs], (bb, tt, fc)).reshape(r, fc)
        um1 = jnp.where(first, p1, pltpu.roll(u, 1, 0))
        um2 = jnp.where(first, p0, jnp.where(second, p1, pltpu.roll(u, 2, 0)))
        carry[:, :, cs] = u.reshape(bb, tt, fc)[:, tt - 2:, :]
        return cb_ref[:, cs] + cw_ref[0:1, cs] * um2 + cw_ref[1:2, cs] * um1 + cw_ref[2:3, cs] * u

    acc = jnp.zeros(x.shape, F32)
    for c in range(ff // fc):
        a = conv(slice(c * fc, (c + 1) * fc))
        gte = conv(slice(ff + c * fc, ff + (c + 1) * fc))
        acc = acc + _dot((_silu(a) * gte).astype(BF16), wdn_ref[c * fc:(c + 1) * fc, :])
    y_ref[...] = _rms(x + acc, gf_ref[...])
    st_out[...] = carry[...]


def _ffn(x2d, g, wup, cw, cb, wdn, state, gfin, *, bb, tt):
    rows, d = x2d.shape
    b, n_keep, ff2 = state.shape
    assert n_keep == 2 and tt >= 2
    ff = ff2 // 2
    fc = FF_CHUNK
    assert ff % fc == 0
    r = bb * tt
    n_t = rows // (b * tt)
    assert rows % r == 0 and b % bb == 0 and (bb == 1 or n_t == 1)
    row = pl.BlockSpec((r, d), lambda gi, ti: (gi * n_t + ti, 0))
    full = lambda a: pl.BlockSpec(a.shape, lambda gi, ti: (0,) * a.ndim)
    st = pl.BlockSpec((bb, n_keep, ff2), lambda gi, ti: (gi, 0, 0))
    kern = functools.partial(_ffn_kernel, bb=bb, tt=tt, ff=ff, fc=fc)
    return pl.pallas_call(
        kern,
        grid=(b // bb, n_t),
        in_specs=[row, full(g), full(wup), full(cw), full(cb), full(wdn), st, full(gfin)],
        out_specs=(row, st),
        out_shape=(jax.ShapeDtypeStruct((rows, d), F32), jax.ShapeDtypeStruct(state.shape, F32)),
        scratch_shapes=[pltpu.VMEM((bb, n_keep, ff2), F32)],
        compiler_params=_params(("parallel", "arbitrary")),
        name="conv_ffn",
    )(x2d, g, wup, cw, cb, wdn, state, gfin)


def _rope_tables(pos, rope, q_scale):
    posf = pos.astype(F32)[:, None]
    half = rope // 2
    mf = 1.0 / (ROPE_BASE ** (jnp.arange(0, rope, 2, dtype=F32) / rope))
    cm, sm = jnp.cos(posf * mf[None, :]), jnp.sin(posf * mf[None, :])
    n = pos.shape[0]
    nope = LANES - 2 * rope
    z = lambda w: jnp.zeros((n, w), F32)
    cq = q_scale * jnp.concatenate([jnp.ones((n, nope), F32), cm, cm, z(rope)], axis=1)
    sq = q_scale * jnp.concatenate([z(nope), sm, sm, z(rope)], axis=1)
    ckr = jnp.concatenate([cm, cm, z(LANES - rope)], axis=1)
    skr = jnp.concatenate([sm, sm, z(LANES - rope)], axis=1)
    rf = 1.0 / (ROPE_BASE ** jnp.linspace(0.0, 1.0, LANES // 2, dtype=F32))
    cr_h, sr_h = jnp.cos(posf * rf[None, :]), jnp.sin(posf * rf[None, :])
    cr = jnp.concatenate([cr_h, cr_h], axis=1)
    sr = jnp.concatenate([-sr_h, sr_h], axis=1)
    del half
    return cq, sq, ckr, skr, cr, sr


def _layer_weights(w_in, w_uq, w_uk, w_uv, q_rank, kv_rank, rope, n_ret):
    d = w_in.shape[0]
    n_mla, qk = w_uq.shape[1:]
    nope = qk - rope
    v_dim = w_uv.shape[2]
    half = rope // 2
    assert nope + 2 * rope == LANES and kv_rank == LANES and w_uk.shape[2] == nope
    ret_w = n_ret * LANES
    o_kv, o_kr = q_rank, q_rank + kv_rank
    o_rq = o_kr + rope
    assert w_in.shape[1] == o_rq + 4 * ret_w
    wkr = w_in[:, o_kr:o_rq]
    wkr_rot = jnp.concatenate([-wkr[:, half:], wkr[:, :half]], axis=1)
    pad = jnp.zeros((d, LANES - rope), w_in.dtype)
    w1 = jnp.concatenate([w_in[:, :o_kr], wkr, pad, wkr_rot, pad, w_in[:, o_rq:]], axis=1).astype(BF16)
    c0 = o_kr
    cols = (0, o_kv, c0, c0 + LANES, c0 + 2 * LANES, c0 + 2 * LANES + ret_w, c0 + 2 * LANES + 2 * ret_w,
            c0 + 2 * LANES + 3 * ret_w, c0 + 2 * LANES + 4 * ret_w)

    zq = lambda w: jnp.zeros((q_rank, n_mla, w), w_uq.dtype)
    r1, r2 = w_uq[..., nope:nope + half], w_uq[..., nope + half:]
    wq = jnp.concatenate([w_uq, zq(rope)], axis=-1).reshape(q_rank, n_mla * LANES)
    wq_rot = jnp.concatenate([zq(nope), -r2, r1, zq(rope)], axis=-1).reshape(q_rank, n_mla * LANES)
    wqq = jnp.concatenate([wq, wq_rot], axis=1).astype(BF16)

    wk_nope = jnp.concatenate([w_uk, jnp.zeros((kv_rank, n_mla, LANES - nope), w_uk.dtype)], axis=-1)
    place = jnp.concatenate([jnp.zeros((rope, nope), F32), jnp.eye(rope, dtype=F32),
                             jnp.zeros((rope, LANES - nope - rope), F32)], axis=1)
    wk_rope = jnp.concatenate([jnp.broadcast_to(place[:, None, :], (rope, n_mla, LANES)),
                               jnp.zeros((LANES - rope, n_mla, LANES), F32)], axis=0)
    wk = jnp.concatenate([wk_nope, wk_rope], axis=0).reshape(kv_rank + LANES, n_mla * LANES).astype(BF16)
    assert n_mla % 2 == 0 and 2 * v_dim == LANES
    zv = jnp.zeros((kv_rank, n_mla // 2, v_dim), w_uv.dtype)
    wv = jnp.stack([jnp.concatenate([w_uv[:, 0::2], zv], axis=-1),
                    jnp.concatenate([zv, w_uv[:, 1::2]], axis=-1)], axis=2)
    wv = wv.reshape(kv_rank, n_mla * LANES).astype(BF16)
    wuk_t = jnp.transpose(w_uk, (1, 2, 0)).astype(BF16)
    wuv_h = jnp.transpose(w_uv, (1, 0, 2)).astype(BF16)
    return w1, cols, wqq, wk, wv, wuk_t, wuv_h, nope, v_dim, n_mla


def kernel(x_prompt, x_sample, cache_mla_ckv, cache_mla_krope, state_ret, state_ffn_conv, cache_mem_k,
           cache_mem_v, mem_prompt, norm_mix_g, w_in, q_norm_g, kv_norm_g, w_uq, w_uk, w_uv, ret_gn_g, w_o,
           norm_mem_g, mem_norm_g, w_cq, w_ck, w_cv, w_co, norm_ffn_g, w_up, conv_w, conv_b, w_down,
           final_norm_g):
    bp, tp, d = x_prompt.shape
    bs, ts, _ = x_sample.shape
    depth = w_in.shape[0]
    past = cache_mla_ckv.shape[2]
    q_rank, kv_rank = w_uq.shape[1], w_uk.shape[1]
    rope = cache_mla_krope.shape[3]
    n_ret, ret_dk, ret_dv = state_ret.shape[2:]
    mem_tok, mem_heads, mem_dh = cache_mem_k.shape[2:]
    ff2 = w_up.shape[2]
    assert conv_w.shape[1] == 3 and ret_dk == LANES
    qk = w_uq.shape[3]
    q_scale = float(qk) ** -0.5 * LOG2E
    log_g = tuple(math.log(1.0 - 2.0 ** (-5.0 - i)) for i in range(n_ret))

    tabs_p = _rope_tables(jnp.arange(tp, dtype=jnp.int32), rope, q_scale)
    tabs_s = _rope_tables(jnp.tile(past + jnp.arange(ts, dtype=jnp.int32), bs), rope, q_scale)

    hp = x_prompt.reshape(bp * tp, d)
    hs = x_sample.reshape(bs * ts, d)
    row2 = lambda a: a.reshape(1, -1)
    outs = {k: [] for k in ("p_ckv", "p_kr", "p_ret", "p_conv", "p_mk", "p_mv", "s_ckv", "s_kr", "s_ret", "s_conv")}
    tile_p = min(ROW_TILE, tp)
    for l in range(depth):
        w1, cols, wqq, wk, wv, wuk_t, wuv_h, nope, v_dim, n_mla = _layer_weights(
            w_in[l], w_uq[l], w_uk[l], w_uv[l], q_rank, kv_rank, rope, n_ret)
        inproj = functools.partial(
            _inproj, g=row2(norm_mix_g[l]), w1=w1, qg=row2(q_norm_g[l]), kvg=row2(kv_norm_g[l]), wqq=wqq,
            wk=wk, wv=wv, cols=cols, n_mla=n_mla, n_ret=n_ret, rope=rope, kv_rank=kv_rank, v_dim=v_dim)
        wo = w_o[l].astype(BF16)
        wcq = w_cq[l].reshape(d, mem_heads * mem_dh).astype(BF16)
        wco = w_co[l].astype(BF16)
        wckv = jnp.concatenate([w_ck[l].reshape(d, -1), w_cv[l].reshape(d, -1)], axis=1).astype(BF16)
        wup = w_up[l].astype(BF16)
        wdn = w_down[l].astype(BF16)
        gn = row2(ret_gn_g[l])
        mix = functools.partial(_mix, wo=wo, g=row2(norm_mem_g[l]), wcq=wcq, wco=wco, n_heads=mem_heads)
        ffn = functools.partial(_ffn, g=row2(norm_ffn_g[l]), wup=wup, cw=conv_w[l], cb=row2(conv_b[l]),
                                wdn=wdn, gfin=row2(final_norm_g))

        q, k, v, ckv, kr, rq, rk, rv, rg = inproj(hp, tabs=tabs_p)
        o_mla = _mla_attn(q.reshape(bp, tp, -1), k.reshape(bp, tp, -1), v.reshape(bp, tp, -1), v_dim=v_dim)
        b3 = lambda a: a.reshape(bp, tp, -1)
        o_ret, ret_fin = _retention(b3(rq), b3(rk), b3(rv), b3(rg),
                                    jnp.zeros((bp, n_ret, ret_dk, ret_dv), F32), gn, log_g)
        mk, mv = _memkv(mem_prompt.reshape(bp * mem_tok, d), row2(mem_norm_g[l]), wckv)
        x2 = mix(hp, o_mla.reshape(bp * tp, -1), o_ret.reshape(bp * tp, -1),
                 mk=mk.reshape(bp, mem_tok, -1), mv=mv.reshape(bp, mem_tok, -1), bb=1, tt=tile_p)
        hp_next, conv_fin = ffn(x2, state=jnp.zeros((bp, 2, ff2), F32), bb=1, tt=tile_p)
        outs["p_ckv"].append(ckv.reshape(bp, tp, kv_rank))
        outs["p_kr"].append(kr.reshape(bp, tp, rope))
        outs["p_ret"].append(ret_fin)
        outs["p_conv"].append(conv_fin)
        outs["p_mk"].append(mk.reshape(bp, mem_tok, mem_heads, mem_dh))
        outs["p_mv"].append(mv.reshape(bp, mem_tok, mem_heads, mem_dh))

        q, k, v, ckv, kr, rq, rk, rv, rg = inproj(hs, tabs=tabs_s)
        o_mla = _dec_attn(q.reshape(bs, ts, -1), cache_mla_ckv[l], cache_mla_krope[l],
                          ckv.reshape(bs, ts, -1), kr.reshape(bs, ts, -1), wuk_t, wuv_h,
                          nope=nope, rope=rope, v_dim=v_dim)
        s3 = lambda a: a.reshape(bs, ts, -1)
        o_ret, ret_fin = _retention(s3(rq), s3(rk), s3(rv), s3(rg), state_ret[l], gn, log_g)
        bb = math.gcd(bs, DEC_MEM_GROUP)
        x2 = mix(hs, o_mla.reshape(bs * ts, -1), o_ret.reshape(bs * ts, -1),
                 mk=cache_mem_k[l].reshape(bs, mem_tok, -1), mv=cache_mem_v[l].reshape(bs, mem_tok, -1),
                 bb=bb, tt=ts)
        hs_next, conv_fin = ffn(x2, state=state_ffn_conv[l], bb=bs, tt=ts)
        outs["s_ckv"].append(ckv.reshape(bs, ts, kv_rank))
        outs["s_kr"].append(kr.reshape(bs, ts, rope))
        outs["s_ret"].append(ret_fin)
        outs["s_conv"].append(conv_fin)
        assert depth == 1
        hp, hs = hp_next, hs_next

    st = lambda name: jnp.stack(outs[name])
    return (hp.reshape(bp, tp, d), hs.reshape(bs, ts, d),
            st("p_ckv"), st("p_kr"), st("p_ret"), st("p_conv"), st("p_mk"), st("p_mv"),
            st("s_ckv"), st("s_kr"), st("s_ret"), st("s_conv"))
```

```python
import functools
import math

import jax
import jax.numpy as jnp
from jax import lax
from jax.experimental import pallas as pl
from jax.experimental.pallas import tpu as pltpu

F32 = jnp.float32
BF16 = jnp.bfloat16

EPS = 1e-6
ROPE_BASE = 10000.0
MLA_CHUNK = 64
LOG2E = 1.4426950408889634
LANES = 128
SUBLANES = 8
NEG_BIG = -1e30

ROW_TILE = 512
ATTN_Q_TILE = 1024
ATTN_K_TILE = 512
RET_BLOCK = 512
RET_CHUNK = 256
FF_CHUNK = 256
DEC_MEM_GROUP = 8
VMEM_LIMIT = 56 * 1024 * 1024


def _nt_dot(a, b):
    return lax.dot_general(a, b, (((1,), (1,)), ((), ())), preferred_element_type=F32)


def _dot(a, b):
    return jnp.dot(a, b, preferred_element_type=F32)


def _rms(x, g):
    return x * lax.rsqrt(jnp.mean(x * x, axis=-1, keepdims=True) + EPS) * g


def _silu(x):
    return x / (1.0 + jnp.exp(-x))


def _params(sem):
    return pltpu.CompilerParams(dimension_semantics=sem, vmem_limit_bytes=VMEM_LIMIT)


def _inproj_kernel(x_ref, g_ref, w1_ref, qg_ref, kvg_ref, wqq_ref, wk_ref, wv_ref, base_ref, off_ref,
                   q_out, k_out, v_out, ckv_out, kr_out, rq_out, rk_out, rv_out, rg_out,
                   *, cols, n_mla, n_ret, nope, rope, v_dim, q_scale, rk_scale):
    o_q, o_kv, o_kr, o_krr, o_rq, o_rk, o_rv, o_rg, o_end = cols
    h = _rms(x_ref[...], g_ref[...]).astype(BF16)

    ca, sa, cra, sra = (base_ref[0, i:i + 1, :] for i in range(4))
    cb, sb, crb, srb = (off_ref[i] for i in range(4))
    cq_t, sq_t = ca * cb - sa * sb, sa * cb + ca * sb
    cr_t, sr_t = cra * crb - sra * srb, sra * crb + cra * srb

    cq = _rms(_dot(h, w1_ref[:, o_q:o_kv]), qg_ref[...]).astype(BF16)
    qq = _dot(cq, wqq_ref[...])
    for hh in range(n_mla):
        a = qq[:, hh * LANES:(hh + 1) * LANES]
        b = qq[:, (n_mla + hh) * LANES:(n_mla + hh + 1) * LANES]
        q_out[:, hh * LANES:(hh + 1) * LANES] = ((a * cq_t + b * sq_t) * q_scale).astype(BF16)

    ckv = _rms(_dot(h, w1_ref[:, o_kv:o_kr]), kvg_ref[...])
    ckv_out[...] = ckv
    kr = _dot(h, w1_ref[:, o_kr:o_krr]) * cq_t + _dot(h, w1_ref[:, o_krr:o_rq]) * sq_t
    kr_out[...] = kr[:, nope:nope + rope]
    ckv_b = ckv.astype(BF16)
    k_out[...] = _dot(jnp.concatenate([ckv_b, kr.astype(BF16)], axis=-1), wk_ref[...]).astype(BF16)
    vlane = lax.broadcasted_iota(jnp.int32, (1, n_mla * LANES), 1) % (2 * LANES)
    ones_col = ((vlane == v_dim) | (vlane == LANES)).astype(F32)
    v_out[...] = (_dot(ckv_b, wv_ref[...]) + ones_col).astype(BF16)

    zq = _dot(h, w1_ref[:, o_rq:o_rk])
    zk = _dot(h, w1_ref[:, o_rk:o_rv])
    for hh in range(n_ret):
        sl = slice(hh * LANES, (hh + 1) * LANES)
        a = zq[:, sl]
        rq_out[:, sl] = (a * cr_t + pltpu.roll(a, LANES // 2, 1) * sr_t).astype(BF16)
        b = zk[:, sl]
        rk_out[:, sl] = ((b * cr_t + pltpu.roll(b, LANES // 2, 1) * sr_t) * rk_scale).astype(BF16)
    rv_out[...] = _dot(h, w1_ref[:, o_rv:o_rg]).astype(BF16)
    rg_out[...] = _silu(_dot(h, w1_ref[:, o_rg:o_end])).astype(BF16)


def _inproj(x2d, g, w1, qg, kvg, wqq, wk, wv, tabs, *, cols, n_mla, n_ret, nope, rope, kv_rank, v_dim,
            q_scale):
    rows, d = x2d.shape
    base_tab, off_tab = tabs
    tm = off_tab.shape[1]
    assert rows % tm == 0
    base_tiles = base_tab.shape[0]
    ret_w = n_ret * LANES
    row = lambda w: pl.BlockSpec((tm, w), lambda i: (i, 0))
    full = lambda a: pl.BlockSpec(a.shape, lambda i: (0,) * a.ndim)
    base = pl.BlockSpec((1,) + base_tab.shape[1:], lambda i: (i % base_tiles, 0, 0))
    out_shape = (
        jax.ShapeDtypeStruct((rows, n_mla * LANES), BF16),
        jax.ShapeDtypeStruct((rows, n_mla * LANES), BF16),
        jax.ShapeDtypeStruct((rows, n_mla * LANES), BF16),
        jax.ShapeDtypeStruct((rows, kv_rank), F32),
        jax.ShapeDtypeStruct((rows, rope), F32),
        jax.ShapeDtypeStruct((rows, ret_w), BF16),
        jax.ShapeDtypeStruct((rows, ret_w), BF16),
        jax.ShapeDtypeStruct((rows, ret_w), BF16),
        jax.ShapeDtypeStruct((rows, ret_w), BF16),
    )
    kern = functools.partial(_inproj_kernel, cols=cols, n_mla=n_mla, n_ret=n_ret, nope=nope, rope=rope,
                             v_dim=v_dim, q_scale=q_scale, rk_scale=float(LANES) ** -0.5)
    return pl.pallas_call(
        kern,
        grid=(rows // tm,),
        in_specs=[row(d), full(g), full(w1), full(qg), full(kvg), full(wqq), full(wk), full(wv), base,
                  full(off_tab)],
        out_specs=tuple(row(s.shape[1]) for s in out_shape),
        out_shape=out_shape,
        compiler_params=_params(("parallel",)),
        name="inproj",
    )(x2d, g, w1, qg, kvg, wqq, wk, wv, base_tab, off_tab)


def _mla_attn_kernel(q_ref, k_ref, v_ref, o_ref, m_sc, a_sc, *, tq, tk, v_dim):
    qi = pl.program_id(2)
    m_sc[...] = jnp.full(m_sc.shape, -jnp.inf, F32)
    a_sc[...] = jnp.zeros(a_sc.shape, F32)
    n_rep = tk // LANES

    def step(kt, r0, r1, bias):
        ks = pl.multiple_of(kt * tk, tk)
        for hh in range(2):
            sl = slice(hh * LANES, (hh + 1) * LANES)
            s = _nt_dot(q_ref[0, r0:r1, sl], k_ref[0, pl.ds(ks, tk), sl])
            if bias is not None:
                s = s + bias
            m_old = m_sc[hh, r0:r1, :]
            m_new = jnp.maximum(m_old, jnp.max(s, axis=-1, keepdims=True))
            p = jnp.exp2(s - jnp.tile(m_new, (1, n_rep)))
            a_sc[hh, r0:r1, :] = (jnp.exp2(m_old - m_new) * a_sc[hh, r0:r1, :]
                                  + _dot(p.astype(BF16), v_ref[0, pl.ds(ks, tk), sl]))
            m_sc[hh, r0:r1, :] = m_new

    n_diag = tq // tk
    n_full = qi * n_diag

    def body(j, c):
        for i in range(n_diag):
            step(j * n_diag + i, 0, tq, None)
        return c

    lax.fori_loop(0, qi, body, 0)
    row_c = lax.broadcasted_iota(jnp.int32, (tk, tk), 0) // MLA_CHUNK
    col_c = lax.broadcasted_iota(jnp.int32, (tk, tk), 1) // MLA_CHUNK
    diag_bias = jnp.where(col_c <= row_c, 0.0, NEG_BIG)
    for i in range(n_diag):
        step(n_full + i, i * tk, (i + 1) * tk, diag_bias)
        if (i + 1) * tk < tq:
            step(n_full + i, (i + 1) * tk, tq, None)
    a0, a1 = a_sc[0], a_sc[1]
    lane = lax.broadcasted_iota(jnp.int32, (tq, LANES), 1)
    o = jnp.where(lane < v_dim, a0 / a0[:, v_dim:v_dim + 1], a1 / a1[:, 0:1])
    o_ref[0] = o.astype(BF16)


def _mla_attn(q, k, v, *, v_dim):
    b, t, hw = q.shape
    n_pair = hw // (2 * LANES)
    assert 2 * v_dim == LANES and v.shape[2] == hw
    tq = min(ATTN_Q_TILE, t)
    tk = min(ATTN_K_TILE, tq)
    assert t % tq == 0 and tq % tk == 0 and tk % MLA_CHUNK == 0
    kern = functools.partial(_mla_attn_kernel, tq=tq, tk=tk, v_dim=v_dim)
    resident = lambda: pl.BlockSpec((1, t, 2 * LANES), lambda bi, j, i: (bi, 0, j),
                                    pipeline_mode=pl.Buffered(1))
    return pl.pallas_call(
        kern,
        grid=(b, n_pair, t // tq),
        in_specs=[pl.BlockSpec((1, tq, 2 * LANES), lambda bi, j, i: (bi, i, j)), resident(), resident()],
        out_specs=pl.BlockSpec((1, tq, LANES), lambda bi, j, i: (bi, i, j)),
        out_shape=jax.ShapeDtypeStruct((b, t, n_pair * LANES), BF16),
        scratch_shapes=[pltpu.VMEM((2, tq, LANES), F32), pltpu.VMEM((2, tq, LANES), F32)],
        compiler_params=_params(("parallel", "parallel", "arbitrary")),
        name="mla_prompt_attn",
    )(q, k, v)


def _dec_attn_kernel(q_ref, cc_ref, kc_ref, cn_ref, kn_ref, wuk_ref, wuv_ref, o_ref, qa_sc, qr_sc,
                     *, n_heads, ts, past, nope, rope, v_dim):
    for hh in range(n_heads):
        qh = q_ref[0, :, hh * LANES:(hh + 1) * LANES]
        qa_sc[hh * ts:(hh + 1) * ts, :] = _dot(qh[:, :nope], wuk_ref[hh]).astype(BF16)
        qr_sc[hh * ts:(hh + 1) * ts, :] = qh[:, nope:nope + rope]
    qa, qr = qa_sc[...], qr_sc[...]
    cc = cc_ref[0].astype(BF16)
    cn = cn_ref[0].astype(BF16)
    s1 = _nt_dot(qa, cc) + _dot(qr, kc_ref[0].astype(BF16))
    s2 = _nt_dot(qa, cn) + _nt_dot(qr, kn_ref[0].astype(BF16))
    rows = n_heads * ts
    qchunk1 = (past + lax.broadcasted_iota(jnp.int32, (rows, past), 0) % ts) // MLA_CHUNK
    kchunk1 = lax.broadcasted_iota(jnp.int32, (rows, past), 1) // MLA_CHUNK
    s1 = jnp.where(kchunk1 <= qchunk1, s1, NEG_BIG)
    qchunk2 = (past + lax.broadcasted_iota(jnp.int32, (rows, ts), 0) % ts) // MLA_CHUNK
    kchunk2 = (past + lax.broadcasted_iota(jnp.int32, (rows, ts), 1)) // MLA_CHUNK
    s2 = jnp.where(kchunk2 <= qchunk2, s2, NEG_BIG)
    m = jnp.maximum(jnp.max(s1, axis=-1, keepdims=True), jnp.max(s2, axis=-1, keepdims=True))
    p1 = jnp.exp2(s1 - m)
    p2 = jnp.exp2(s2 - m)
    l = jnp.sum(p1, axis=-1, keepdims=True) + jnp.sum(p2, axis=-1, keepdims=True)
    lat = ((_dot(p1.astype(BF16), cc) + _dot(p2.astype(BF16), cn)) / l).astype(BF16)
    per_slab = LANES // v_dim
    for j in range(n_heads // per_slab):
        parts = [_dot(lat[(j * per_slab + i) * ts:(j * per_slab + i + 1) * ts, :], wuv_ref[j * per_slab + i])
                 for i in range(per_slab)]
        o_ref[0, :, j * LANES:(j + 1) * LANES] = jnp.concatenate(parts, axis=-1).astype(BF16)


def _dec_attn(q, ckv_cache, kr_cache, ckv_new, kr_new, wuk_t, wuv_h, *, nope, rope, v_dim):
    b, ts, hw = q.shape
    n_heads = hw // LANES
    past, kv_rank = ckv_cache.shape[1:]
    kern = functools.partial(_dec_attn_kernel, n_heads=n_heads, ts=ts, past=past, nope=nope, rope=rope,
                             v_dim=v_dim)
    per_b = lambda a: pl.BlockSpec((1,) + a.shape[1:], lambda bi: (bi,) + (0,) * (a.ndim - 1))
    full = lambda a: pl.BlockSpec(a.shape, lambda bi: (0,) * a.ndim)
    return pl.pallas_call(
        kern,
        grid=(b,),
        in_specs=[per_b(q), per_b(ckv_cache), per_b(kr_cache), per_b(ckv_new), per_b(kr_new),
                  full(wuk_t), full(wuv_h)],
        out_specs=pl.BlockSpec((1, ts, n_heads * v_dim), lambda bi: (bi, 0, 0)),
        out_shape=jax.ShapeDtypeStruct((b, ts, n_heads * v_dim), BF16),
        scratch_shapes=[pltpu.VMEM((n_heads * ts, kv_rank), BF16), pltpu.VMEM((n_heads * ts, rope), BF16)],
        compiler_params=_params(("parallel",)),
        name="mla_decode_attn",
    )(q, ckv_cache, kr_cache, ckv_new, kr_new, wuk_t, wuv_h)


def _ret_kernel(rq_ref, rk_ref, rv_ref, gt_ref, s0_ref, g_ref, o_ref, sfin_ref, s_sc,
                *, lc, n_sub, log_g):
    t = pl.program_id(1)

    @pl.when(t == 0)
    def _():
        s_sc[...] = s0_ref[0]

    diff = (lax.broadcasted_iota(jnp.int32, (lc, lc), 0)
            - lax.broadcasted_iota(jnp.int32, (lc, lc), 1)).astype(F32)
    pos = lax.broadcasted_iota(jnp.int32, (lc, 1), 0).astype(F32)
    for hh, lg in enumerate(log_g):
        sl = slice(hh * LANES, (hh + 1) * LANES)
        dmask = jnp.where(diff >= 0, jnp.exp(lg * jnp.maximum(diff, 0.0)), 0.0)
        q_dec = jnp.exp(lg * (pos + 1.0))
        k_dec = jnp.exp(lg * (lc - 1.0 - pos))
        s_dec = math.exp(lg * lc)
        gain = g_ref[:, sl]
        for c in range(n_sub):
            rows = slice(c * lc, (c + 1) * lc)
            q = rq_ref[0, rows, sl]
            k = rk_ref[0, rows, sl]
            v = rv_ref[0, rows, sl]
            s_prev = s_sc[hh]
            inner = _nt_dot(q, k) * dmask
            o = _dot(inner.astype(BF16), v)
            o = o + _dot((q.astype(F32) * q_dec).astype(BF16), s_prev.astype(BF16))
            kd = (k.astype(F32) * k_dec).astype(BF16)
            s_sc[hh] = s_dec * s_prev + lax.dot_general(kd, v, (((0,), (0,)), ((), ())),
                                                        preferred_element_type=F32)
            mu = jnp.mean(o, axis=-1, keepdims=True)
            oc = o - mu
            var = jnp.mean(oc * oc, axis=-1, keepdims=True)
            y = oc * lax.rsqrt(var + EPS) * gain * gt_ref[0, rows, sl].astype(F32)
            o_ref[0, rows, sl] = y.astype(BF16)

    @pl.when(t == pl.num_programs(1) - 1)
    def _():
        sfin_ref[0] = s_sc[...]


def _retention(rq, rk, rv, gate, s0, gn_g, log_g):
    b, t, w = rq.shape
    n_heads, dk, dv = s0.shape[1:]
    lb = min(RET_BLOCK, t)
    lc = min(RET_CHUNK, lb)
    assert t % lb == 0 and lb % lc == 0 and dk == LANES and dv == LANES
    kern = functools.partial(_ret_kernel, lc=lc, n_sub=lb // lc, log_g=log_g)
    blk = pl.BlockSpec((1, lb, w), lambda bi, ti: (bi, ti, 0))
    st = pl.BlockSpec((1, n_heads, dk, dv), lambda bi, ti: (bi, 0, 0, 0))
    return pl.pallas_call(
        kern,
        grid=(b, t // lb),
        in_specs=[blk, blk, blk, blk, st, pl.BlockSpec(gn_g.shape, lambda bi, ti: (0, 0))],
        out_specs=(blk, st),
        out_shape=(jax.ShapeDtypeStruct((b, t, w), BF16), jax.ShapeDtypeStruct(s0.shape, F32)),
        scratch_shapes=[pltpu.VMEM((n_heads, dk, dv), F32)],
        compiler_params=_params(("parallel", "arbitrary")),
        name="retention",
    )(rq, rk, rv, gate, s0, gn_g)


def _memkv_kernel(m_ref, g_ref, w_ref, k_out, v_out):
    h = _rms(m_ref[...], g_ref[...]).astype(BF16)
    n = k_out.shape[1]
    k_out[...] = _dot(h, w_ref[:, :n])
    v_out[...] = _dot(h, w_ref[:, n:])


def _memkv(mem2d, g, wkv):
    rows, d = mem2d.shape
    n = wkv.shape[1] // 2
    tm = min(ROW_TILE, rows)
    assert rows % tm == 0
    row = lambda w: pl.BlockSpec((tm, w), lambda i: (i, 0))
    full = lambda a: pl.BlockSpec(a.shape, lambda i: (0,) * a.ndim)
    return pl.pallas_call(
        _memkv_kernel,
        grid=(rows // tm,),
        in_specs=[row(d), full(g), full(wkv)],
        out_specs=(row(n), row(n)),
        out_shape=(jax.ShapeDtypeStruct((rows, n), F32), jax.ShapeDtypeStruct((rows, n), F32)),
        compiler_params=_params(("parallel",)),
        name="mem_kv",
    )(mem2d, g, wkv)


def _mix_kernel(x_ref, om_ref, or_ref, wo_ref, g_ref, wcq_ref, wco_ref, mk_ref, mv_ref, o_ref, qc_sc, oc_sc,
                *, bb, tt, n_heads, dh):
    mixed = jnp.concatenate([om_ref[...], or_ref[...]], axis=-1)
    x1 = x_ref[...] + _dot(mixed, wo_ref[...])
    h = _rms(x1, g_ref[...]).astype(BF16)
    qc_sc[...] = (_dot(h, wcq_ref[...]) * (float(dh) ** -0.5)).astype(BF16)
    for bi in range(bb):
        rows = slice(bi * tt, (bi + 1) * tt)
        for hh in range(n_heads):
            sl = slice(hh * dh, (hh + 1) * dh)
            s = _nt_dot(qc_sc[rows, sl], mk_ref[bi, :, sl].astype(BF16))
            p = jnp.exp(s - jnp.max(s, axis=-1, keepdims=True))
            l = jnp.sum(p, axis=-1, keepdims=True)
            o = _dot(p.astype(BF16), mv_ref[bi, :, sl].astype(BF16)) / l
            oc_sc[rows, sl] = o.astype(BF16)
    o_ref[...] = x1 + _dot(oc_sc[...], wco_ref[...])


def _mix(x2d, om, orr, wo, g, wcq, mk, mv, wco, *, bb, tt, n_heads):
    rows, d = x2d.shape
    b, m_tok, _ = mk.shape
    r = bb * tt
    n_t = rows // (b * tt)
    assert rows % r == 0 and b % bb == 0 and (bb == 1 or n_t == 1)
    row = lambda w: pl.BlockSpec((r, w), lambda gi, ti: (gi * n_t + ti, 0))
    full = lambda a: pl.BlockSpec(a.shape, lambda gi, ti: (0,) * a.ndim)
    mem = pl.BlockSpec((bb, m_tok, d), lambda gi, ti: (gi, 0, 0))
    kern = functools.partial(_mix_kernel, bb=bb, tt=tt, n_heads=n_heads, dh=d // n_heads)
    return pl.pallas_call(
        kern,
        grid=(b // bb, n_t),
        in_specs=[row(d), row(om.shape[1]), row(orr.shape[1]), full(wo), full(g), full(wcq), full(wco),
                  mem, mem],
        out_specs=row(d),
        out_shape=jax.ShapeDtypeStruct((rows, d), F32),
        scratch_shapes=[pltpu.VMEM((r, d), BF16), pltpu.VMEM((r, d), BF16)],
        compiler_params=_params(("parallel", "arbitrary")),
        name="mix_out_mem_attn",
    )(x2d, om, orr, wo, g, wcq, wco, mk, mv)


def _ffn_kernel(x_ref, g_ref, wup_ref, cw_ref, cb_ref, wdn_ref, st_ref, gf_ref, y_ref, st_out, carry,
                ext_a, ext_g, *, bb, tt, ff, fc):
    t = pl.program_id(1)

    @pl.when(t == 0)
    def _():
        carry[...] = st_ref[...]

    r = bb * tt
    x = x_ref[...]
    h = _rms(x, g_ref[...]).astype(BF16)
    halo = ext_a.shape[1] - tt

    def conv(cs, ext):
        u = _dot(h, wup_ref[:, cs])
        ext[:, halo:, :] = u.reshape(bb, tt, fc)
        ext[:, halo - 2:halo, :] = carry[:, :, cs]
        um2 = ext[:, halo - 2:halo - 2 + tt, :].reshape(r, fc)
        um1 = ext[:, halo - 1:halo - 1 + tt, :].reshape(r, fc)
        carry[:, :, cs] = ext[:, halo + tt - 2:, :]
        return cb_ref[:, cs] + cw_ref[0:1, cs] * um2 + cw_ref[1:2, cs] * um1 + cw_ref[2:3, cs] * u

    acc = jnp.zeros(x.shape, F32)
    for c in range(ff // fc):
        a = conv(slice(c * fc, (c + 1) * fc), ext_a)
        gte = conv(slice(ff + c * fc, ff + (c + 1) * fc), ext_g)
        acc = acc + _dot((_silu(a) * gte).astype(BF16), wdn_ref[c * fc:(c + 1) * fc, :])
    y_ref[...] = _rms(x + acc, gf_ref[...])
    st_out[...] = carry[...]


def _ffn(x2d, g, wup, cw, cb, wdn, state, gfin, *, bb, tt):
    rows, d = x2d.shape
    b, n_keep, ff2 = state.shape
    assert n_keep == 2 and tt >= 2
    ff = ff2 // 2
    fc = FF_CHUNK
    assert ff % fc == 0
    r = bb * tt
    n_t = rows // (b * tt)
    assert rows % r == 0 and b % bb == 0 and (bb == 1 or n_t == 1)
    row = pl.BlockSpec((r, d), lambda gi, ti: (gi * n_t + ti, 0))
    full = lambda a: pl.BlockSpec(a.shape, lambda gi, ti: (0,) * a.ndim)
    st = pl.BlockSpec((bb, n_keep, ff2), lambda gi, ti: (gi, 0, 0))
    kern = functools.partial(_ffn_kernel, bb=bb, tt=tt, ff=ff, fc=fc)
    return pl.pallas_call(
        kern,
        grid=(b // bb, n_t),
        in_specs=[row, full(g), full(wup), full(cw), full(cb), full(wdn), st, full(gfin)],
        out_specs=(row, st),
        out_shape=(jax.ShapeDtypeStruct((rows, d), F32), jax.ShapeDtypeStruct(state.shape, F32)),
        scratch_shapes=[pltpu.VMEM((bb, n_keep, ff2), F32),
                        pltpu.VMEM((bb, SUBLANES + tt, fc), F32), pltpu.VMEM((bb, SUBLANES + tt, fc), F32)],
        compiler_params=_params(("parallel", "arbitrary")),
        name="conv_ffn",
    )(x2d, g, wup, cw, cb, wdn, state, gfin)


def _rope_tables(base_pos, off_pos, rope):
    nope = LANES - 2 * rope
    mf = 1.0 / (ROPE_BASE ** (jnp.arange(0, rope, 2, dtype=F32) / rope))
    f_mla = jnp.concatenate([jnp.zeros((nope,), F32), mf, mf, jnp.zeros((rope,), F32)])
    rf = 1.0 / (ROPE_BASE ** jnp.linspace(0.0, 1.0, LANES // 2, dtype=F32))
    f_ret = jnp.concatenate([rf, rf])
    sign = jnp.concatenate([-jnp.ones((LANES // 2,), F32), jnp.ones((LANES // 2,), F32)])

    def tab(pos):
        p = pos.astype(F32)[:, None]
        return jnp.stack([jnp.cos(p * f_mla), jnp.sin(p * f_mla), jnp.cos(p * f_ret), sign * jnp.sin(p * f_ret)])

    return jnp.transpose(tab(base_pos), (1, 0, 2)), tab(off_pos)


def _layer_weights(w_in, w_uq, w_uk, w_uv, q_rank, kv_rank, rope, n_ret):
    d = w_in.shape[0]
    n_mla, qk = w_uq.shape[1:]
    nope = qk - rope
    v_dim = w_uv.shape[2]
    half = rope // 2
    assert nope + 2 * rope == LANES and kv_rank == LANES and w_uk.shape[2] == nope
    ret_w = n_ret * LANES
    o_kv, o_kr = q_rank, q_rank + kv_rank
    o_rq = o_kr + rope
    assert w_in.shape[1] == o_rq + 4 * ret_w
    wkr = w_in[:, o_kr:o_rq]
    wkr_rot = jnp.concatenate([-wkr[:, half:], wkr[:, :half]], axis=1)
    zl, zr = jnp.zeros((d, nope), w_in.dtype), jnp.zeros((d, rope), w_in.dtype)
    w1 = jnp.concatenate([w_in[:, :o_kr], zl, wkr, zr, zl, wkr_rot, zr, w_in[:, o_rq:]], axis=1).astype(BF16)
    c0 = o_kr
    cols = (0, o_kv, c0, c0 + LANES, c0 + 2 * LANES, c0 + 2 * LANES + ret_w, c0 + 2 * LANES + 2 * ret_w,
            c0 + 2 * LANES + 3 * ret_w, c0 + 2 * LANES + 4 * ret_w)

    zq = lambda w: jnp.zeros((q_rank, n_mla, w), w_uq.dtype)
    r1, r2 = w_uq[..., nope:nope + half], w_uq[..., nope + half:]
    wq = jnp.concatenate([w_uq, zq(rope)], axis=-1).reshape(q_rank, n_mla * LANES)
    wq_rot = jnp.concatenate([zq(nope), -r2, r1, zq(rope)], axis=-1).reshape(q_rank, n_mla * LANES)
    wqq = jnp.concatenate([wq, wq_rot], axis=1).astype(BF16)

    wk_nope = jnp.concatenate([w_uk, jnp.zeros((kv_rank, n_mla, LANES - nope), w_uk.dtype)], axis=-1)
    lane = jnp.arange(LANES)
    live = (lane >= nope) & (lane < nope + rope)
    place = jnp.where(live[:, None], jnp.eye(LANES, dtype=F32), 0.0)
    wk_rope = jnp.broadcast_to(place[:, None, :], (LANES, n_mla, LANES))
    wk = jnp.concatenate([wk_nope, wk_rope], axis=0).reshape(kv_rank + LANES, n_mla * LANES).astype(BF16)
    assert n_mla % 2 == 0 and 2 * v_dim == LANES
    zv = jnp.zeros((kv_rank, n_mla // 2, v_dim), w_uv.dtype)
    wv = jnp.stack([jnp.concatenate([w_uv[:, 0::2], zv], axis=-1),
                    jnp.concatenate([zv, w_uv[:, 1::2]], axis=-1)], axis=2)
    wv = wv.reshape(kv_rank, n_mla * LANES).astype(BF16)
    wuk_t = jnp.transpose(w_uk, (1, 2, 0)).astype(BF16)
    wuv_h = jnp.transpose(w_uv, (1, 0, 2)).astype(BF16)
    return w1, cols, wqq, wk, wv, wuk_t, wuv_h, nope, v_dim, n_mla


def kernel(x_prompt, x_sample, cache_mla_ckv, cache_mla_krope, state_ret, state_ffn_conv, cache_mem_k,
           cache_mem_v, mem_prompt, norm_mix_g, w_in, q_norm_g, kv_norm_g, w_uq, w_uk, w_uv, ret_gn_g, w_o,
           norm_mem_g, mem_norm_g, w_cq, w_ck, w_cv, w_co, norm_ffn_g, w_up, conv_w, conv_b, w_down,
           final_norm_g):
    bp, tp, d = x_prompt.shape
    bs, ts, _ = x_sample.shape
    depth = w_in.shape[0]
    past = cache_mla_ckv.shape[2]
    q_rank, kv_rank = w_uq.shape[1], w_uk.shape[1]
    rope = cache_mla_krope.shape[3]
    n_ret, ret_dk, ret_dv = state_ret.shape[2:]
    mem_tok, mem_heads, mem_dh = cache_mem_k.shape[2:]
    ff2 = w_up.shape[2]
    assert conv_w.shape[1] == 3 and ret_dk == LANES
    qk = w_uq.shape[3]
    q_scale = float(qk) ** -0.5 * LOG2E
    log_g = tuple(math.log(1.0 - 2.0 ** (-5.0 - i)) for i in range(n_ret))

    tile_p = min(ROW_TILE, tp)
    tile_s = min(ROW_TILE, bs * ts)
    assert tp % tile_p == 0 and tile_s % ts == 0
    tabs_p = _rope_tables(jnp.arange(0, tp, tile_p), jnp.arange(tile_p), rope)
    tabs_s = _rope_tables(jnp.full((1,), past), jnp.arange(tile_s) % ts, rope)

    hp = x_prompt.reshape(bp * tp, d)
    hs = x_sample.reshape(bs * ts, d)
    row2 = lambda a: a.reshape(1, -1)
    outs = {k: [] for k in ("p_ckv", "p_kr", "p_ret", "p_conv", "p_mk", "p_mv", "s_ckv", "s_kr", "s_ret", "s_conv")}
    for l in range(depth):
        w1, cols, wqq, wk, wv, wuk_t, wuv_h, nope, v_dim, n_mla = _layer_weights(
            w_in[l], w_uq[l], w_uk[l], w_uv[l], q_rank, kv_rank, rope, n_ret)
        inproj = functools.partial(
            _inproj, g=row2(norm_mix_g[l]), w1=w1, qg=row2(q_norm_g[l]), kvg=row2(kv_norm_g[l]), wqq=wqq,
            wk=wk, wv=wv, cols=cols, n_mla=n_mla, n_ret=n_ret, nope=nope, rope=rope, kv_rank=kv_rank,
            v_dim=v_dim, q_scale=q_scale)
        wo = w_o[l].astype(BF16)
        wcq = w_cq[l].reshape(d, mem_heads * mem_dh).astype(BF16)
        wco = w_co[l].astype(BF16)
        wckv = jnp.concatenate([w_ck[l].reshape(d, -1), w_cv[l].reshape(d, -1)], axis=1).astype(BF16)
        wup = w_up[l].astype(BF16)
        wdn = w_down[l].astype(BF16)
        gn = row2(ret_gn_g[l])
        mix = functools.partial(_mix, wo=wo, g=row2(norm_mem_g[l]), wcq=wcq, wco=wco, n_heads=mem_heads)
        ffn = functools.partial(_ffn, g=row2(norm_ffn_g[l]), wup=wup, cw=conv_w[l], cb=row2(conv_b[l]),
                                wdn=wdn, gfin=row2(final_norm_g))

        q, k, v, ckv, kr, rq, rk, rv, rg = inproj(hp, tabs=tabs_p)
        o_mla = _mla_attn(q.reshape(bp, tp, -1), k.reshape(bp, tp, -1), v.reshape(bp, tp, -1), v_dim=v_dim)
        b3 = lambda a: a.reshape(bp, tp, -1)
        o_ret, ret_fin = _retention(b3(rq), b3(rk), b3(rv), b3(rg),
                                    jnp.zeros((bp, n_ret, ret_dk, ret_dv), F32), gn, log_g)
        mk, mv = _memkv(mem_prompt.reshape(bp * mem_tok, d), row2(mem_norm_g[l]), wckv)
        x2 = mix(hp, o_mla.reshape(bp * tp, -1), o_ret.reshape(bp * tp, -1),
                 mk=mk.reshape(bp, mem_tok, -1), mv=mv.reshape(bp, mem_tok, -1), bb=1, tt=tile_p)
        hp_next, conv_fin = ffn(x2, state=jnp.zeros((bp, 2, ff2), F32), bb=1, tt=tile_p)
        outs["p_ckv"].append(ckv.reshape(bp, tp, kv_rank))
        outs["p_kr"].append(kr.reshape(bp, tp, rope))
        outs["p_ret"].append(ret_fin)
        outs["p_conv"].append(conv_fin)
        outs["p_mk"].append(mk.reshape(bp, mem_tok, mem_heads, mem_dh))
        outs["p_mv"].append(mv.reshape(bp, mem_tok, mem_heads, mem_dh))

        q, k, v, ckv, kr, rq, rk, rv, rg = inproj(hs, tabs=tabs_s)
        o_mla = _dec_attn(q.reshape(bs, ts, -1), cache_mla_ckv[l], jnp.swapaxes(cache_mla_krope[l], 1, 2),
                          ckv.reshape(bs, ts, -1), kr.reshape(bs, ts, -1), wuk_t, wuv_h,
                          nope=nope, rope=rope, v_dim=v_dim)
        s3 = lambda a: a.reshape(bs, ts, -1)
        o_ret, ret_fin = _retention(s3(rq), s3(rk), s3(rv), s3(rg), state_ret[l], gn, log_g)
        bb = math.gcd(bs, DEC_MEM_GROUP)
        x2 = mix(hs, o_mla.reshape(bs * ts, -1), o_ret.reshape(bs * ts, -1),
                 mk=cache_mem_k[l].reshape(bs, mem_tok, -1), mv=cache_mem_v[l].reshape(bs, mem_tok, -1),
                 bb=bb, tt=ts)
        hs_next, conv_fin = ffn(x2, state=state_ffn_conv[l], bb=bs, tt=ts)
        outs["s_ckv"].append(ckv.reshape(bs, ts, kv_rank))
        outs["s_kr"].append(kr.reshape(bs, ts, rope))
        outs["s_ret"].append(ret_fin)
        outs["s_conv"].append(conv_fin)
        assert depth == 1
        hp, hs = hp_next, hs_next

    st = lambda name: jnp.stack(outs[name])
    return (hp.reshape(bp, tp, d), hs.reshape(bs, ts, d),
            st("p_ckv"), st("p_kr"), st("p_ret"), st("p_conv"), st("p_mk"), st("p_mv"),
            st("s_ckv"), st("s_kr"), st("s_ret"), st("s_conv"))
```

```python
import functools
import math

import jax
import jax.numpy as jnp
from jax import lax
from jax.experimental import pallas as pl
from jax.experimental.pallas import tpu as pltpu

F32 = jnp.float32
BF16 = jnp.bfloat16

EPS = 1e-6
ROPE_BASE = 10000.0
MLA_CHUNK = 64
LOG2E = 1.4426950408889634
LANES = 128
SUBLANES = 8
NEG_BIG = -1e30

ROW_TILE = 512
FFN_TILE = 1024
ATTN_Q_TILE = 2048
ATTN_K_TILE = 512
RET_BLOCK = 512
RET_CHUNK = 256
FF_CHUNK = 256
DEC_MEM_GROUP = 8
VMEM_LIMIT = 56 * 1024 * 1024


def _nt_dot(a, b):
    return lax.dot_general(a, b, (((1,), (1,)), ((), ())), preferred_element_type=F32)


def _dot(a, b):
    return jnp.dot(a, b, preferred_element_type=F32)


def _rms(x, g):
    return x * lax.rsqrt(jnp.mean(x * x, axis=-1, keepdims=True) + EPS) * g


def _silu(x):
    return x / (1.0 + jnp.exp(-x))


def _params(sem):
    return pltpu.CompilerParams(dimension_semantics=sem, vmem_limit_bytes=VMEM_LIMIT)


def _inproj_kernel(x_ref, g_ref, w1_ref, qg_ref, kvg_ref, wqq_ref, wk_ref, wv_ref, base_ref, off_ref,
                   q_out, k_out, v_out, ckv_out, kr_out, rq_out, rk_out, rv_out, rg_out,
                   *, cols, n_mla, n_ret, nope, rope, v_dim, q_scale, rk_scale):
    o_q, o_kv, o_kr, o_krr, o_rq, o_rk, o_rv, o_rg, o_end = cols
    h = _rms(x_ref[...], g_ref[...]).astype(BF16)

    ca, sa, cra, sra = (base_ref[0, i:i + 1, :] for i in range(4))
    cb, sb, crb, srb = (off_ref[i] for i in range(4))
    cq_t, sq_t = ca * cb - sa * sb, sa * cb + ca * sb
    cr_t, sr_t = cra * crb - sra * srb, sra * crb + cra * srb

    cq = _rms(_dot(h, w1_ref[:, o_q:o_kv]), qg_ref[...]).astype(BF16)
    qq = _dot(cq, wqq_ref[...])
    for hh in range(n_mla):
        a = qq[:, hh * LANES:(hh + 1) * LANES]
        b = qq[:, (n_mla + hh) * LANES:(n_mla + hh + 1) * LANES]
        q_out[:, hh * LANES:(hh + 1) * LANES] = ((a * cq_t + b * sq_t) * q_scale).astype(BF16)

    ckv = _rms(_dot(h, w1_ref[:, o_kv:o_kr]), kvg_ref[...])
    ckv_out[...] = ckv
    kr = _dot(h, w1_ref[:, o_kr:o_krr]) * cq_t + _dot(h, w1_ref[:, o_krr:o_rq]) * sq_t
    kr_out[...] = kr[:, nope:nope + rope]
    ckv_b = ckv.astype(BF16)
    k_out[...] = _dot(jnp.concatenate([ckv_b, kr.astype(BF16)], axis=-1), wk_ref[...]).astype(BF16)
    vlane = lax.broadcasted_iota(jnp.int32, (1, n_mla * LANES), 1) % (2 * LANES)
    ones_col = ((vlane == v_dim) | (vlane == LANES)).astype(F32)
    v_out[...] = (_dot(ckv_b, wv_ref[...]) + ones_col).astype(BF16)

    zq = _dot(h, w1_ref[:, o_rq:o_rk])
    zk = _dot(h, w1_ref[:, o_rk:o_rv])
    for hh in range(n_ret):
        sl = slice(hh * LANES, (hh + 1) * LANES)
        a = zq[:, sl]
        rq_out[:, sl] = (a * cr_t + pltpu.roll(a, LANES // 2, 1) * sr_t).astype(BF16)
        b = zk[:, sl]
        rk_out[:, sl] = ((b * cr_t + pltpu.roll(b, LANES // 2, 1) * sr_t) * rk_scale).astype(BF16)
    rv_out[...] = _dot(h, w1_ref[:, o_rv:o_rg]).astype(BF16)
    rg_out[...] = _silu(_dot(h, w1_ref[:, o_rg:o_end])).astype(BF16)


def _inproj(x2d, g, w1, qg, kvg, wqq, wk, wv, tabs, *, cols, n_mla, n_ret, nope, rope, kv_rank, v_dim,
            q_scale):
    rows, d = x2d.shape
    base_tab, off_tab = tabs
    tm = off_tab.shape[1]
    assert rows % tm == 0
    base_tiles = base_tab.shape[0]
    ret_w = n_ret * LANES
    row = lambda w: pl.BlockSpec((tm, w), lambda i: (i, 0))
    full = lambda a: pl.BlockSpec(a.shape, lambda i: (0,) * a.ndim)
    base = pl.BlockSpec((1,) + base_tab.shape[1:], lambda i: (i % base_tiles, 0, 0))
    out_shape = (
        jax.ShapeDtypeStruct((rows, n_mla * LANES), BF16),
        jax.ShapeDtypeStruct((rows, n_mla * LANES), BF16),
        jax.ShapeDtypeStruct((rows, n_mla * LANES), BF16),
        jax.ShapeDtypeStruct((rows, kv_rank), F32),
        jax.ShapeDtypeStruct((rows, rope), F32),
        jax.ShapeDtypeStruct((rows, ret_w), BF16),
        jax.ShapeDtypeStruct((rows, ret_w), BF16),
        jax.ShapeDtypeStruct((rows, ret_w), BF16),
        jax.ShapeDtypeStruct((rows, ret_w), BF16),
    )
    kern = functools.partial(_inproj_kernel, cols=cols, n_mla=n_mla, n_ret=n_ret, nope=nope, rope=rope,
                             v_dim=v_dim, q_scale=q_scale, rk_scale=float(LANES) ** -0.5)
    return pl.pallas_call(
        kern,
        grid=(rows // tm,),
        in_specs=[row(d), full(g), full(w1), full(qg), full(kvg), full(wqq), full(wk), full(wv), base,
                  full(off_tab)],
        out_specs=tuple(row(s.shape[1]) for s in out_shape),
        out_shape=out_shape,
        compiler_params=_params(("parallel",)),
        name="inproj",
    )(x2d, g, w1, qg, kvg, wqq, wk, wv, base_tab, off_tab)


def _mla_attn_kernel(q_ref, k_ref, v_ref, o_ref, m_sc, a_sc, *, tq, tk, v_dim):
    qi = pl.program_id(2)
    m_sc[...] = jnp.full(m_sc.shape, -jnp.inf, F32)
    a_sc[...] = jnp.zeros(a_sc.shape, F32)
    n_rep = tk // LANES

    def step(kt, r0, r1, bias):
        ks = pl.multiple_of(kt * tk, tk)
        for hh in range(2):
            sl = slice(hh * LANES, (hh + 1) * LANES)
            s = _nt_dot(q_ref[0, r0:r1, sl], k_ref[0, pl.ds(ks, tk), sl])
            if bias is not None:
                s = s + bias
            m_old = m_sc[hh, r0:r1, :]
            m_new = jnp.maximum(m_old, jnp.max(s, axis=-1, keepdims=True))
            p = jnp.exp2(s - jnp.tile(m_new, (1, n_rep)))
            a_sc[hh, r0:r1, :] = (jnp.exp2(m_old - m_new) * a_sc[hh, r0:r1, :]
                                  + _dot(p.astype(BF16), v_ref[0, pl.ds(ks, tk), sl]))
            m_sc[hh, r0:r1, :] = m_new

    n_diag = tq // tk
    n_full = qi * n_diag

    def body(j, c):
        for i in range(n_diag):
            step(j * n_diag + i, 0, tq, None)
        return c

    lax.fori_loop(0, qi, body, 0)
    row_c = lax.broadcasted_iota(jnp.int32, (tk, tk), 0) // MLA_CHUNK
    col_c = lax.broadcasted_iota(jnp.int32, (tk, tk), 1) // MLA_CHUNK
    diag_bias = jnp.where(col_c <= row_c, 0.0, NEG_BIG)
    for i in range(n_diag):
        step(n_full + i, i * tk, (i + 1) * tk, diag_bias)
        if (i + 1) * tk < tq:
            step(n_full + i, (i + 1) * tk, tq, None)
    a0, a1 = a_sc[0], a_sc[1]
    lane = lax.broadcasted_iota(jnp.int32, (tq, LANES), 1)
    o = jnp.where(lane < v_dim, a0 / a0[:, v_dim:v_dim + 1], a1 / a1[:, 0:1])
    o_ref[0] = o.astype(BF16)


def _mla_attn(q, k, v, *, v_dim):
    b, t, hw = q.shape
    n_pair = hw // (2 * LANES)
    assert 2 * v_dim == LANES and v.shape[2] == hw
    tq = min(ATTN_Q_TILE, t)
    tk = min(ATTN_K_TILE, tq)
    assert t % tq == 0 and tq % tk == 0 and tk % MLA_CHUNK == 0
    kern = functools.partial(_mla_attn_kernel, tq=tq, tk=tk, v_dim=v_dim)
    resident = lambda: pl.BlockSpec((1, t, 2 * LANES), lambda bi, j, i: (bi, 0, j))
    return pl.pallas_call(
        kern,
        grid=(b, n_pair, t // tq),
        in_specs=[pl.BlockSpec((1, tq, 2 * LANES), lambda bi, j, i: (bi, i, j)), resident(), resident()],
        out_specs=pl.BlockSpec((1, tq, LANES), lambda bi, j, i: (bi, i, j)),
        out_shape=jax.ShapeDtypeStruct((b, t, n_pair * LANES), BF16),
        scratch_shapes=[pltpu.VMEM((2, tq, LANES), F32), pltpu.VMEM((2, tq, LANES), F32)],
        compiler_params=_params(("parallel", "parallel", "arbitrary")),
        name="mla_prompt_attn",
    )(q, k, v)


def _dec_attn_kernel(q_ref, cc_ref, kc_ref, cn_ref, kn_ref, wuk_ref, wuv_ref, o_ref, qa_sc, qr_sc,
                     *, n_heads, ts, past, nope, rope, v_dim):
    for hh in range(n_heads):
        qh = q_ref[0, :, hh * LANES:(hh + 1) * LANES]
        qa_sc[hh * ts:(hh + 1) * ts, :] = _dot(qh[:, :nope], wuk_ref[hh]).astype(BF16)
        qr_sc[hh * ts:(hh + 1) * ts, :] = qh[:, nope:nope + rope]
    qa, qr = qa_sc[...], qr_sc[...]
    cc = cc_ref[0].astype(BF16)
    cn = cn_ref[0].astype(BF16)
    s1 = _nt_dot(qa, cc) + _dot(qr, kc_ref[0].astype(BF16))
    s2 = _nt_dot(qa, cn) + _nt_dot(qr, kn_ref[0].astype(BF16))
    rows = n_heads * ts
    qchunk1 = (past + lax.broadcasted_iota(jnp.int32, (rows, past), 0) % ts) // MLA_CHUNK
    kchunk1 = lax.broadcasted_iota(jnp.int32, (rows, past), 1) // MLA_CHUNK
    s1 = jnp.where(kchunk1 <= qchunk1, s1, NEG_BIG)
    qchunk2 = (past + lax.broadcasted_iota(jnp.int32, (rows, ts), 0) % ts) // MLA_CHUNK
    kchunk2 = (past + lax.broadcasted_iota(jnp.int32, (rows, ts), 1)) // MLA_CHUNK
    s2 = jnp.where(kchunk2 <= qchunk2, s2, NEG_BIG)
    m = jnp.maximum(jnp.max(s1, axis=-1, keepdims=True), jnp.max(s2, axis=-1, keepdims=True))
    p1 = jnp.exp2(s1 - m)
    p2 = jnp.exp2(s2 - m)
    l = jnp.sum(p1, axis=-1, keepdims=True) + jnp.sum(p2, axis=-1, keepdims=True)
    lat = ((_dot(p1.astype(BF16), cc) + _dot(p2.astype(BF16), cn)) / l).astype(BF16)
    per_slab = LANES // v_dim
    for j in range(n_heads // per_slab):
        parts = [_dot(lat[(j * per_slab + i) * ts:(j * per_slab + i + 1) * ts, :], wuv_ref[j * per_slab + i])
                 for i in range(per_slab)]
        o_ref[0, :, j * LANES:(j + 1) * LANES] = jnp.concatenate(parts, axis=-1).astype(BF16)


def _dec_attn(q, ckv_cache, kr_cache, ckv_new, kr_new, wuk_t, wuv_h, *, nope, rope, v_dim):
    b, ts, hw = q.shape
    n_heads = hw // LANES
    past, kv_rank = ckv_cache.shape[1:]
    kern = functools.partial(_dec_attn_kernel, n_heads=n_heads, ts=ts, past=past, nope=nope, rope=rope,
                             v_dim=v_dim)
    per_b = lambda a: pl.BlockSpec((1,) + a.shape[1:], lambda bi: (bi,) + (0,) * (a.ndim - 1))
    full = lambda a: pl.BlockSpec(a.shape, lambda bi: (0,) * a.ndim)
    return pl.pallas_call(
        kern,
        grid=(b,),
        in_specs=[per_b(q), per_b(ckv_cache), per_b(kr_cache), per_b(ckv_new), per_b(kr_new),
                  full(wuk_t), full(wuv_h)],
        out_specs=pl.BlockSpec((1, ts, n_heads * v_dim), lambda bi: (bi, 0, 0)),
        out_shape=jax.ShapeDtypeStruct((b, ts, n_heads * v_dim), BF16),
        scratch_shapes=[pltpu.VMEM((n_heads * ts, kv_rank), BF16), pltpu.VMEM((n_heads * ts, rope), BF16)],
        compiler_params=_params(("parallel",)),
        name="mla_decode_attn",
    )(q, ckv_cache, kr_cache, ckv_new, kr_new, wuk_t, wuv_h)


def _ret_kernel(rq_ref, rk_ref, rv_ref, gt_ref, s0_ref, g_ref, o_ref, sfin_ref, s_sc,
                *, lc, n_sub, log_g):
    t = pl.program_id(1)

    @pl.when(t == 0)
    def _():
        s_sc[...] = s0_ref[0]

    diff = (lax.broadcasted_iota(jnp.int32, (lc, lc), 0)
            - lax.broadcasted_iota(jnp.int32, (lc, lc), 1)).astype(F32)
    pos = lax.broadcasted_iota(jnp.int32, (lc, 1), 0).astype(F32)
    for hh, lg in enumerate(log_g):
        sl = slice(hh * LANES, (hh + 1) * LANES)
        dmask = jnp.where(diff >= 0, jnp.exp(lg * jnp.maximum(diff, 0.0)), 0.0)
        q_dec = jnp.exp(lg * (pos + 1.0))
        k_dec = jnp.exp(lg * (lc - 1.0 - pos))
        s_dec = math.exp(lg * lc)
        gain = g_ref[:, sl]
        for c in range(n_sub):
            rows = slice(c * lc, (c + 1) * lc)
            q = rq_ref[0, rows, sl]
            k = rk_ref[0, rows, sl]
            v = rv_ref[0, rows, sl]
            s_prev = s_sc[hh]
            inner = _nt_dot(q, k) * dmask
            o = _dot(inner.astype(BF16), v)
            o = o + _dot((q.astype(F32) * q_dec).astype(BF16), s_prev.astype(BF16))
            kd = (k.astype(F32) * k_dec).astype(BF16)
            s_sc[hh] = s_dec * s_prev + lax.dot_general(kd, v, (((0,), (0,)), ((), ())),
                                                        preferred_element_type=F32)
            mu = jnp.mean(o, axis=-1, keepdims=True)
            oc = o - mu
            var = jnp.mean(oc * oc, axis=-1, keepdims=True)
            y = oc * lax.rsqrt(var + EPS) * gain * gt_ref[0, rows, sl].astype(F32)
            o_ref[0, rows, sl] = y.astype(BF16)

    @pl.when(t == pl.num_programs(1) - 1)
    def _():
        sfin_ref[0] = s_sc[...]


def _retention(rq, rk, rv, gate, s0, gn_g, log_g):
    b, t, w = rq.shape
    n_heads, dk, dv = s0.shape[1:]
    lb = min(RET_BLOCK, t)
    lc = min(RET_CHUNK, lb)
    assert t % lb == 0 and lb % lc == 0 and dk == LANES and dv == LANES
    kern = functools.partial(_ret_kernel, lc=lc, n_sub=lb // lc, log_g=log_g)
    blk = pl.BlockSpec((1, lb, w), lambda bi, ti: (bi, ti, 0))
    st = pl.BlockSpec((1, n_heads, dk, dv), lambda bi, ti: (bi, 0, 0, 0))
    return pl.pallas_call(
        kern,
        grid=(b, t // lb),
        in_specs=[blk, blk, blk, blk, st, pl.BlockSpec(gn_g.shape, lambda bi, ti: (0, 0))],
        out_specs=(blk, st),
        out_shape=(jax.ShapeDtypeStruct((b, t, w), BF16), jax.ShapeDtypeStruct(s0.shape, F32)),
        scratch_shapes=[pltpu.VMEM((n_heads, dk, dv), F32)],
        compiler_params=_params(("parallel", "arbitrary")),
        name="retention",
    )(rq, rk, rv, gate, s0, gn_g)


def _memkv_kernel(m_ref, g_ref, w_ref, k_out, v_out):
    h = _rms(m_ref[...], g_ref[...]).astype(BF16)
    n = k_out.shape[1]
    k_out[...] = _dot(h, w_ref[:, :n])
    v_out[...] = _dot(h, w_ref[:, n:])


def _memkv(mem2d, g, wkv):
    rows, d = mem2d.shape
    n = wkv.shape[1] // 2
    tm = min(ROW_TILE, rows)
    assert rows % tm == 0
    row = lambda w: pl.BlockSpec((tm, w), lambda i: (i, 0))
    full = lambda a: pl.BlockSpec(a.shape, lambda i: (0,) * a.ndim)
    return pl.pallas_call(
        _memkv_kernel,
        grid=(rows // tm,),
        in_specs=[row(d), full(g), full(wkv)],
        out_specs=(row(n), row(n)),
        out_shape=(jax.ShapeDtypeStruct((rows, n), F32), jax.ShapeDtypeStruct((rows, n), F32)),
        compiler_params=_params(("parallel",)),
        name="mem_kv",
    )(mem2d, g, wkv)


def _mix_kernel(x_ref, om_ref, or_ref, wo_ref, g_ref, wcq_ref, wco_ref, mk_ref, mv_ref, o_ref, qc_sc, oc_sc,
                *, bb, tt, n_heads, dh):
    mixed = jnp.concatenate([om_ref[...], or_ref[...]], axis=-1)
    x1 = x_ref[...] + _dot(mixed, wo_ref[...])
    h = _rms(x1, g_ref[...]).astype(BF16)
    qc_sc[...] = (_dot(h, wcq_ref[...]) * (float(dh) ** -0.5)).astype(BF16)
    for bi in range(bb):
        rows = slice(bi * tt, (bi + 1) * tt)
        for hh in range(n_heads):
            sl = slice(hh * dh, (hh + 1) * dh)
            s = _nt_dot(qc_sc[rows, sl], mk_ref[bi, :, sl].astype(BF16))
            p = jnp.exp(s - jnp.max(s, axis=-1, keepdims=True))
            l = jnp.sum(p, axis=-1, keepdims=True)
            o = _dot(p.astype(BF16), mv_ref[bi, :, sl].astype(BF16)) / l
            oc_sc[rows, sl] = o.astype(BF16)
    o_ref[...] = x1 + _dot(oc_sc[...], wco_ref[...])


def _mix(x2d, om, orr, wo, g, wcq, mk, mv, wco, *, bb, tt, n_heads):
    rows, d = x2d.shape
    b = mk.shape[0]
    r = bb * tt
    n_t = rows // (b * tt)
    assert rows % r == 0 and b % bb == 0 and (bb == 1 or n_t == 1) and mk.shape == mv.shape
    row = lambda w: pl.BlockSpec((r, w), lambda gi, ti: (gi * n_t + ti, 0))
    full = lambda a: pl.BlockSpec(a.shape, lambda gi, ti: (0,) * a.ndim)
    mem = pl.BlockSpec((bb,) + mk.shape[1:], lambda gi, ti: (gi,) + (0,) * (mk.ndim - 1))
    kern = functools.partial(_mix_kernel, bb=bb, tt=tt, n_heads=n_heads, dh=d // n_heads)
    return pl.pallas_call(
        kern,
        grid=(b // bb, n_t),
        in_specs=[row(d), row(om.shape[1]), row(orr.shape[1]), full(wo), full(g), full(wcq), full(wco),
                  mem, mem],
        out_specs=row(d),
        out_shape=jax.ShapeDtypeStruct((rows, d), F32),
        scratch_shapes=[pltpu.VMEM((r, d), BF16), pltpu.VMEM((r, d), BF16)],
        compiler_params=_params(("parallel", "arbitrary")),
        name="mix_out_mem_attn",
    )(x2d, om, orr, wo, g, wcq, wco, mk, mv)


def _ffn_kernel(x_ref, g_ref, wup_ref, cw_ref, cb_ref, wdn_ref, st_ref, gf_ref, y_ref, st_out, carry,
                ext_a, ext_g, *, bb, tt, ff, fc):
    t = pl.program_id(1)

    @pl.when(t == 0)
    def _():
        carry[...] = st_ref[...]

    r = bb * tt
    x = x_ref[...]
    h = _rms(x, g_ref[...]).astype(BF16)
    halo = ext_a.shape[1] - tt

    def conv(cs, ext):
        u = _dot(h, wup_ref[:, cs])
        ext[:, halo:, :] = u.reshape(bb, tt, fc)
        ext[:, halo - 2:halo, :] = carry[:, :, cs]
        um2 = ext[:, halo - 2:halo - 2 + tt, :].reshape(r, fc)
        um1 = ext[:, halo - 1:halo - 1 + tt, :].reshape(r, fc)
        carry[:, :, cs] = ext[:, halo + tt - 2:, :]
        return cb_ref[:, cs] + cw_ref[0:1, cs] * um2 + cw_ref[1:2, cs] * um1 + cw_ref[2:3, cs] * u

    acc = jnp.zeros(x.shape, F32)
    for c in range(ff // fc):
        a = conv(slice(c * fc, (c + 1) * fc), ext_a)
        gte = conv(slice(ff + c * fc, ff + (c + 1) * fc), ext_g)
        acc = acc + _dot((_silu(a) * gte).astype(BF16), wdn_ref[c * fc:(c + 1) * fc, :])
    y_ref[...] = _rms(x + acc, gf_ref[...])
    st_out[...] = carry[...]


SEG = SUBLANES * SUBLANES


def _ffn_seq_kernel(x_ref, g_ref, wup_ref, cw_ref, cb_ref, wdn_ref, st_ref, gf_ref, y_ref, st_out, carry,
                    slab, act, *, tt, ff, fc):
    t = pl.program_id(1)
    d = x_ref.shape[1]
    n_seg, n_slab = tt // SEG, d // LANES

    @pl.when(t == 0)
    def _():
        for j in range(2):
            carry[j] = jnp.broadcast_to(st_ref[0, j:j + 1, :], carry.shape[1:])

    for k in range(n_slab):
        for c in range(n_seg):
            for a in range(SUBLANES):
                src = x_ref[SEG * c + SUBLANES * a:SEG * c + SUBLANES * (a + 1), k * LANES:(k + 1) * LANES]
                slab[k, pl.ds(SEG * c + a, SUBLANES, stride=SUBLANES), :] = src
    x = jnp.concatenate([slab[k] for k in range(n_slab)], axis=-1)
    h = _rms(x, g_ref[...]).astype(BF16)
    sub0 = lax.broadcasted_iota(jnp.int32, (n_seg, SUBLANES, fc), 1) == 0

    def conv(cs):
        u = _dot(h, wup_ref[:, cs]).reshape(n_seg, SUBLANES, SUBLANES, fc)

        def shifted(j):
            prev = jnp.concatenate([carry[j - 6, :, cs][None], u[:-1, j]], axis=0)
            return jnp.where(sub0, pltpu.roll(prev, 1, 1), pltpu.roll(u[:, j], 1, 1))[:, None]

        r6, r7 = shifted(6), shifted(7)
        tap1 = jnp.concatenate([r7, u[:, :SUBLANES - 1]], axis=1)
        tap2 = jnp.concatenate([r6, r7, u[:, :SUBLANES - 2]], axis=1)
        carry[0, :, cs] = u[n_seg - 1, 6]
        carry[1, :, cs] = u[n_seg - 1, 7]
        cv = cb_ref[:, cs] + cw_ref[0:1, cs] * tap2 + cw_ref[1:2, cs] * tap1 + cw_ref[2:3, cs] * u
        return cv.reshape(tt, fc)

    for c in range(ff // fc):
        a = conv(slice(c * fc, (c + 1) * fc))
        gte = conv(slice(ff + c * fc, ff + (c + 1) * fc))
        act[:, c * fc:(c + 1) * fc] = (_silu(a) * gte).astype(BF16)
    y = _rms(x + _dot(act[...], wdn_ref[...]), gf_ref[...])

    for k in range(n_slab):
        slab[k] = y[:, k * LANES:(k + 1) * LANES]
    for k in range(n_slab):
        for c in range(n_seg):
            for a in range(SUBLANES):
                y_ref[SEG * c + SUBLANES * a:SEG * c + SUBLANES * (a + 1), k * LANES:(k + 1) * LANES] = (
                    slab[k, pl.ds(SEG * c + a, SUBLANES, stride=SUBLANES), :])
    for j in range(2):
        st_out[0, j:j + 1, :] = carry[j, SUBLANES - 1:SUBLANES, :]


def _ffn(x2d, g, wup, cw, cb, wdn, state, gfin, *, bb, tt):
    rows, d = x2d.shape
    b, n_keep, ff2 = state.shape
    assert n_keep == 2 and tt >= 2
    ff = ff2 // 2
    fc = FF_CHUNK
    assert ff % fc == 0
    r = bb * tt
    n_t = rows // (b * tt)
    assert rows % r == 0 and b % bb == 0 and (bb == 1 or n_t == 1)
    row = pl.BlockSpec((r, d), lambda gi, ti: (gi * n_t + ti, 0))
    full = lambda a: pl.BlockSpec(a.shape, lambda gi, ti: (0,) * a.ndim)
    st = pl.BlockSpec((bb, n_keep, ff2), lambda gi, ti: (gi, 0, 0))
    if bb == 1 and tt % SEG == 0 and d % LANES == 0:
        kern = functools.partial(_ffn_seq_kernel, tt=tt, ff=ff, fc=fc)
        scratch = [pltpu.VMEM((n_keep, SUBLANES, ff2), F32), pltpu.VMEM((d // LANES, tt, LANES), F32),
                   pltpu.VMEM((tt, ff), BF16)]
    else:
        kern = functools.partial(_ffn_kernel, bb=bb, tt=tt, ff=ff, fc=fc)
        scratch = [pltpu.VMEM((bb, n_keep, ff2), F32),
                   pltpu.VMEM((bb, SUBLANES + tt, fc), F32), pltpu.VMEM((bb, SUBLANES + tt, fc), F32)]
    return pl.pallas_call(
        kern,
        grid=(b // bb, n_t),
        in_specs=[row, full(g), full(wup), full(cw), full(cb), full(wdn), st, full(gfin)],
        out_specs=(row, st),
        out_shape=(jax.ShapeDtypeStruct((rows, d), F32), jax.ShapeDtypeStruct(state.shape, F32)),
        scratch_shapes=scratch,
        compiler_params=_params(("parallel", "arbitrary")),
        name="conv_ffn",
    )(x2d, g, wup, cw, cb, wdn, state, gfin)


def _rope_tables(base_pos, off_pos, rope):
    nope = LANES - 2 * rope
    mf = 1.0 / (ROPE_BASE ** (jnp.arange(0, rope, 2, dtype=F32) / rope))
    f_mla = jnp.concatenate([jnp.zeros((nope,), F32), mf, mf, jnp.zeros((rope,), F32)])
    rf = 1.0 / (ROPE_BASE ** jnp.linspace(0.0, 1.0, LANES // 2, dtype=F32))
    f_ret = jnp.concatenate([rf, rf])
    sign = jnp.concatenate([-jnp.ones((LANES // 2,), F32), jnp.ones((LANES // 2,), F32)])

    def tab(pos):
        p = pos.astype(F32)[:, None]
        return jnp.stack([jnp.cos(p * f_mla), jnp.sin(p * f_mla), jnp.cos(p * f_ret), sign * jnp.sin(p * f_ret)])

    return jnp.transpose(tab(base_pos), (1, 0, 2)), tab(off_pos)


def _layer_weights(w_in, w_uq, w_uk, w_uv, q_rank, kv_rank, rope, n_ret):
    d = w_in.shape[0]
    n_mla, qk = w_uq.shape[1:]
    nope = qk - rope
    v_dim = w_uv.shape[2]
    half = rope // 2
    assert nope + 2 * rope == LANES and kv_rank == LANES and w_uk.shape[2] == nope
    ret_w = n_ret * LANES
    o_kv, o_kr = q_rank, q_rank + kv_rank
    o_rq = o_kr + rope
    assert w_in.shape[1] == o_rq + 4 * ret_w
    wkr = w_in[:, o_kr:o_rq]
    wkr_rot = jnp.concatenate([-wkr[:, half:], wkr[:, :half]], axis=1)
    zl, zr = jnp.zeros((d, nope), w_in.dtype), jnp.zeros((d, rope), w_in.dtype)
    w1 = jnp.concatenate([w_in[:, :o_kr], zl, wkr, zr, zl, wkr_rot, zr, w_in[:, o_rq:]], axis=1).astype(BF16)
    c0 = o_kr
    cols = (0, o_kv, c0, c0 + LANES, c0 + 2 * LANES, c0 + 2 * LANES + ret_w, c0 + 2 * LANES + 2 * ret_w,
            c0 + 2 * LANES + 3 * ret_w, c0 + 2 * LANES + 4 * ret_w)

    zq = lambda w: jnp.zeros((q_rank, n_mla, w), w_uq.dtype)
    r1, r2 = w_uq[..., nope:nope + half], w_uq[..., nope + half:]
    wq = jnp.concatenate([w_uq, zq(rope)], axis=-1).reshape(q_rank, n_mla * LANES)
    wq_rot = jnp.concatenate([zq(nope), -r2, r1, zq(rope)], axis=-1).reshape(q_rank, n_mla * LANES)
    wqq = jnp.concatenate([wq, wq_rot], axis=1).astype(BF16)

    wk_nope = jnp.concatenate([w_uk, jnp.zeros((kv_rank, n_mla, LANES - nope), w_uk.dtype)], axis=-1)
    lane = jnp.arange(LANES)
    live = (lane >= nope) & (lane < nope + rope)
    place = jnp.where(live[:, None], jnp.eye(LANES, dtype=F32), 0.0)
    wk_rope = jnp.broadcast_to(place[:, None, :], (LANES, n_mla, LANES))
    wk = jnp.concatenate([wk_nope, wk_rope], axis=0).reshape(kv_rank + LANES, n_mla * LANES).astype(BF16)
    assert n_mla % 2 == 0 and 2 * v_dim == LANES
    zv = jnp.zeros((kv_rank, n_mla // 2, v_dim), w_uv.dtype)
    wv = jnp.stack([jnp.concatenate([w_uv[:, 0::2], zv], axis=-1),
                    jnp.concatenate([zv, w_uv[:, 1::2]], axis=-1)], axis=2)
    wv = wv.reshape(kv_rank, n_mla * LANES).astype(BF16)
    wuk_t = jnp.transpose(w_uk, (1, 2, 0)).astype(BF16)
    wuv_h = jnp.transpose(w_uv, (1, 0, 2)).astype(BF16)
    return w1, cols, wqq, wk, wv, wuk_t, wuv_h, nope, v_dim, n_mla


def kernel(x_prompt, x_sample, cache_mla_ckv, cache_mla_krope, state_ret, state_ffn_conv, cache_mem_k,
           cache_mem_v, mem_prompt, norm_mix_g, w_in, q_norm_g, kv_norm_g, w_uq, w_uk, w_uv, ret_gn_g, w_o,
           norm_mem_g, mem_norm_g, w_cq, w_ck, w_cv, w_co, norm_ffn_g, w_up, conv_w, conv_b, w_down,
           final_norm_g):
    bp, tp, d = x_prompt.shape
    bs, ts, _ = x_sample.shape
    depth = w_in.shape[0]
    past = cache_mla_ckv.shape[2]
    q_rank, kv_rank = w_uq.shape[1], w_uk.shape[1]
    rope = cache_mla_krope.shape[3]
    n_ret, ret_dk, ret_dv = state_ret.shape[2:]
    mem_tok, mem_heads, mem_dh = cache_mem_k.shape[2:]
    ff2 = w_up.shape[2]
    assert conv_w.shape[1] == 3 and ret_dk == LANES
    qk = w_uq.shape[3]
    q_scale = float(qk) ** -0.5 * LOG2E
    log_g = tuple(math.log(1.0 - 2.0 ** (-5.0 - i)) for i in range(n_ret))

    tile_p = min(ROW_TILE, tp)
    tile_s = min(ROW_TILE, bs * ts)
    assert tp % tile_p == 0 and tile_s % ts == 0
    tabs_p = _rope_tables(jnp.arange(0, tp, tile_p), jnp.arange(tile_p), rope)
    tabs_s = _rope_tables(jnp.full((1,), past), jnp.arange(tile_s) % ts, rope)

    hp = x_prompt.reshape(bp * tp, d)
    hs = x_sample.reshape(bs * ts, d)
    row2 = lambda a: a.reshape(1, -1)
    outs = {k: [] for k in ("p_ckv", "p_kr", "p_ret", "p_conv", "p_mk", "p_mv", "s_ckv", "s_kr", "s_ret", "s_conv")}
    for l in range(depth):
        w1, cols, wqq, wk, wv, wuk_t, wuv_h, nope, v_dim, n_mla = _layer_weights(
            w_in[l], w_uq[l], w_uk[l], w_uv[l], q_rank, kv_rank, rope, n_ret)
        inproj = functools.partial(
            _inproj, g=row2(norm_mix_g[l]), w1=w1, qg=row2(q_norm_g[l]), kvg=row2(kv_norm_g[l]), wqq=wqq,
            wk=wk, wv=wv, cols=cols, n_mla=n_mla, n_ret=n_ret, nope=nope, rope=rope, kv_rank=kv_rank,
            v_dim=v_dim, q_scale=q_scale)
        wo = w_o[l].astype(BF16)
        wcq = w_cq[l].reshape(d, mem_heads * mem_dh).astype(BF16)
        wco = w_co[l].astype(BF16)
        wckv = jnp.concatenate([w_ck[l].reshape(d, -1), w_cv[l].reshape(d, -1)], axis=1).astype(BF16)
        wup = w_up[l].astype(BF16)
        wdn = w_down[l].astype(BF16)
        gn = row2(ret_gn_g[l])
        mix = functools.partial(_mix, wo=wo, g=row2(norm_mem_g[l]), wcq=wcq, wco=wco, n_heads=mem_heads)
        ffn = functools.partial(_ffn, g=row2(norm_ffn_g[l]), wup=wup, cw=conv_w[l], cb=row2(conv_b[l]),
                                wdn=wdn, gfin=row2(final_norm_g))

        q, k, v, ckv, kr, rq, rk, rv, rg = inproj(hp, tabs=tabs_p)
        o_mla = _mla_attn(q.reshape(bp, tp, -1), k.reshape(bp, tp, -1), v.reshape(bp, tp, -1), v_dim=v_dim)
        b3 = lambda a: a.reshape(bp, tp, -1)
        o_ret, ret_fin = _retention(b3(rq), b3(rk), b3(rv), b3(rg),
                                    jnp.zeros((bp, n_ret, ret_dk, ret_dv), F32), gn, log_g)
        mk, mv = _memkv(mem_prompt.reshape(bp * mem_tok, d), row2(mem_norm_g[l]), wckv)
        x2 = mix(hp, o_mla.reshape(bp * tp, -1), o_ret.reshape(bp * tp, -1),
                 mk=mk.reshape(bp, mem_tok, -1), mv=mv.reshape(bp, mem_tok, -1), bb=1, tt=tile_p)
        hp_next, conv_fin = ffn(x2, state=jnp.zeros((bp, 2, ff2), F32), bb=1, tt=math.gcd(FFN_TILE, tp))
        outs["p_ckv"].append(ckv.reshape(bp, tp, kv_rank))
        outs["p_kr"].append(kr.reshape(bp, tp, rope))
        outs["p_ret"].append(ret_fin)
        outs["p_conv"].append(conv_fin)
        outs["p_mk"].append(mk.reshape(bp, mem_tok, mem_heads, mem_dh))
        outs["p_mv"].append(mv.reshape(bp, mem_tok, mem_heads, mem_dh))

        q, k, v, ckv, kr, rq, rk, rv, rg = inproj(hs, tabs=tabs_s)
        o_mla = _dec_attn(q.reshape(bs, ts, -1), cache_mla_ckv[l], jnp.swapaxes(cache_mla_krope[l], 1, 2),
                          ckv.reshape(bs, ts, -1), kr.reshape(bs, ts, -1), wuk_t, wuv_h,
                          nope=nope, rope=rope, v_dim=v_dim)
        s3 = lambda a: a.reshape(bs, ts, -1)
        o_ret, ret_fin = _retention(s3(rq), s3(rk), s3(rv), s3(rg), state_ret[l], gn, log_g)
        bb = math.gcd(bs, DEC_MEM_GROUP)
        x2 = mix(hs, o_mla.reshape(bs * ts, -1), o_ret.reshape(bs * ts, -1),
                 mk=cache_mem_k[l].reshape(bs, mem_tok, -1), mv=cache_mem_v[l].reshape(bs, mem_tok, -1),
                 bb=bb, tt=ts)
        hs_next, conv_fin = ffn(x2, state=state_ffn_conv[l], bb=bs, tt=ts)
        outs["s_ckv"].append(ckv.reshape(bs, ts, kv_rank))
        outs["s_kr"].append(kr.reshape(bs, ts, rope))
        outs["s_ret"].append(ret_fin)
        outs["s_conv"].append(conv_fin)
        assert depth == 1
        hp, hs = hp_next, hs_next

    st = lambda name: jnp.stack(outs[name])
    return (hp.reshape(bp, tp, d), hs.reshape(bs, ts, d),
            st("p_ckv"), st("p_kr"), st("p_ret"), st("p_conv"), st("p_mk"), st("p_mv"),
            st("s_ckv"), st("s_kr"), st("s_ret"), st("s_conv"))
```

```python
import functools
import math

import jax
import jax.numpy as jnp
from jax import lax
from jax.experimental import pallas as pl
from jax.experimental.pallas import tpu as pltpu

F32 = jnp.float32
BF16 = jnp.bfloat16

EPS = 1e-6
ROPE_BASE = 10000.0
MLA_CHUNK = 64
LOG2E = 1.4426950408889634
LANES = 128
SUBLANES = 8
NEG_BIG = -1e30

ROW_TILE = 512
FFN_TILE = 1024
ATTN_Q_TILE = 2048
ATTN_K_TILE = 512
RET_BLOCK = 512
RET_CHUNK = 256
FF_CHUNK = 256
DEC_MEM_GROUP = 8
VMEM_LIMIT = 56 * 1024 * 1024


def _nt_dot(a, b):
    return lax.dot_general(a, b, (((1,), (1,)), ((), ())), preferred_element_type=F32)


def _dot(a, b):
    return jnp.dot(a, b, preferred_element_type=F32)


def _rms(x, g):
    return x * lax.rsqrt(jnp.mean(x * x, axis=-1, keepdims=True) + EPS) * g


def _silu(x):
    return x / (1.0 + jnp.exp(-x))


def _params(sem):
    return pltpu.CompilerParams(dimension_semantics=sem, vmem_limit_bytes=VMEM_LIMIT)


def _inproj_kernel(x_ref, g_ref, w1_ref, qg_ref, kvg_ref, wqq_ref, wk_ref, wv_ref, base_ref, off_ref,
                   q_out, k_out, v_out, ckv_out, kr_out, rq_out, rk_out, rv_out, rg_out,
                   *, cols, n_mla, n_ret, nope, rope, v_dim, q_scale, rk_scale):
    o_q, o_kv, o_kr, o_krr, o_rq, o_rk, o_rv, o_rg, o_end = cols
    h = _rms(x_ref[...], g_ref[...]).astype(BF16)

    ca, sa, cra, sra = (base_ref[0, i:i + 1, :] for i in range(4))
    cb, sb, crb, srb = (off_ref[i] for i in range(4))
    cq_t, sq_t = ca * cb - sa * sb, sa * cb + ca * sb
    cr_t, sr_t = cra * crb - sra * srb, sra * crb + cra * srb

    cq = _rms(_dot(h, w1_ref[:, o_q:o_kv]), qg_ref[...]).astype(BF16)
    qq = _dot(cq, wqq_ref[...])
    for hh in range(n_mla):
        a = qq[:, hh * LANES:(hh + 1) * LANES]
        b = qq[:, (n_mla + hh) * LANES:(n_mla + hh + 1) * LANES]
        q_out[:, hh * LANES:(hh + 1) * LANES] = ((a * cq_t + b * sq_t) * q_scale).astype(BF16)

    ckv = _rms(_dot(h, w1_ref[:, o_kv:o_kr]), kvg_ref[...])
    ckv_out[...] = ckv
    kr = _dot(h, w1_ref[:, o_kr:o_krr]) * cq_t + _dot(h, w1_ref[:, o_krr:o_rq]) * sq_t
    kr_out[...] = kr[:, nope:nope + rope]
    ckv_b = ckv.astype(BF16)
    k_out[...] = _dot(jnp.concatenate([ckv_b, kr.astype(BF16)], axis=-1), wk_ref[...]).astype(BF16)
    vlane = lax.broadcasted_iota(jnp.int32, (1, n_mla * LANES), 1) % (2 * LANES)
    ones_col = ((vlane == v_dim) | (vlane == LANES)).astype(F32)
    v_slabs = _dot(ckv_b, wv_ref[...]) + ones_col
    v_out[0, :, 0, :, :] = v_slabs.T.astype(BF16).reshape(n_mla // 2, 2 * LANES, v_slabs.shape[0])

    zq = _dot(h, w1_ref[:, o_rq:o_rk])
    zk = _dot(h, w1_ref[:, o_rk:o_rv])
    for hh in range(n_ret):
        sl = slice(hh * LANES, (hh + 1) * LANES)
        a = zq[:, sl]
        rq_out[:, sl] = (a * cr_t + pltpu.roll(a, LANES // 2, 1) * sr_t).astype(BF16)
        b = zk[:, sl]
        rk_out[:, sl] = ((b * cr_t + pltpu.roll(b, LANES // 2, 1) * sr_t) * rk_scale).astype(BF16)
    rv_out[...] = _dot(h, w1_ref[:, o_rv:o_rg]).astype(BF16)
    rg_out[...] = _silu(_dot(h, w1_ref[:, o_rg:o_end])).astype(BF16)


def _inproj(x2d, g, w1, qg, kvg, wqq, wk, wv, tabs, *, cols, n_mla, n_ret, nope, rope, kv_rank, v_dim,
            q_scale):
    rows, d = x2d.shape
    base_tab, off_tab = tabs
    tm = off_tab.shape[1]
    assert rows % tm == 0
    base_tiles = base_tab.shape[0]
    ret_w = n_ret * LANES
    row = lambda w: pl.BlockSpec((tm, w), lambda i: (i, 0))
    full = lambda a: pl.BlockSpec(a.shape, lambda i: (0,) * a.ndim)
    base = pl.BlockSpec((1,) + base_tab.shape[1:], lambda i: (i % base_tiles, 0, 0))
    out_shape = (
        jax.ShapeDtypeStruct((rows, n_mla * LANES), BF16),
        jax.ShapeDtypeStruct((rows, n_mla * LANES), BF16),
        jax.ShapeDtypeStruct((rows // (tm * base_tiles), n_mla // 2, base_tiles, 2 * LANES, tm), BF16),
        jax.ShapeDtypeStruct((rows, kv_rank), F32),
        jax.ShapeDtypeStruct((rows, rope), F32),
        jax.ShapeDtypeStruct((rows, ret_w), BF16),
        jax.ShapeDtypeStruct((rows, ret_w), BF16),
        jax.ShapeDtypeStruct((rows, ret_w), BF16),
        jax.ShapeDtypeStruct((rows, ret_w), BF16),
    )
    kern = functools.partial(_inproj_kernel, cols=cols, n_mla=n_mla, n_ret=n_ret, nope=nope, rope=rope,
                             v_dim=v_dim, q_scale=q_scale, rk_scale=float(LANES) ** -0.5)
    return pl.pallas_call(
        kern,
        grid=(rows // tm,),
        in_specs=[row(d), full(g), full(w1), full(qg), full(kvg), full(wqq), full(wk), full(wv), base,
                  full(off_tab)],
        out_specs=tuple(
            row(s.shape[1]) if len(s.shape) == 2 else
            pl.BlockSpec((1, s.shape[1], 1) + s.shape[3:], lambda i: (i // base_tiles, 0, i % base_tiles, 0, 0))
            for s in out_shape),
        out_shape=out_shape,
        compiler_params=_params(("parallel",)),
        name="inproj",
    )(x2d, g, w1, qg, kvg, wqq, wk, wv, base_tab, off_tab)


def _mla_attn_kernel(q_ref, k_ref, vt_ref, o_ref, m_sc, a_sc, *, tq, tk, v_dim):
    qi = pl.program_id(2)
    m_sc[...] = jnp.full(m_sc.shape, -jnp.inf, F32)
    a_sc[...] = jnp.zeros(a_sc.shape, F32)

    def step(kt, c0, c1, bias):
        ks = pl.multiple_of(kt * tk, tk)
        for hh in range(2):
            sl = slice(hh * LANES, (hh + 1) * LANES)
            s = _nt_dot(k_ref[0, pl.ds(ks, tk), sl], q_ref[0, c0:c1, sl])
            if bias is not None:
                s = s + bias
            m_old = m_sc[hh, :, c0:c1]
            m_new = jnp.maximum(m_old, jnp.max(s, axis=0, keepdims=True))
            p = jnp.exp2(s - jnp.tile(m_new, (tk // SUBLANES, 1)))
            alpha = jnp.tile(jnp.exp2(m_old - m_new), (LANES // SUBLANES, 1))
            a_sc[hh, :, c0:c1] = alpha * a_sc[hh, :, c0:c1] + _dot(vt_ref[0, 0, kt, sl, :], p.astype(BF16))
            m_sc[hh, :, c0:c1] = m_new

    n_diag = tq // tk
    n_full = qi * n_diag

    def body(j, c):
        for i in range(n_diag):
            step(j * n_diag + i, 0, tq, None)
        return c

    lax.fori_loop(0, qi, body, 0)
    key_c = lax.broadcasted_iota(jnp.int32, (tk, tq), 0) // MLA_CHUNK
    qry_c = lax.broadcasted_iota(jnp.int32, (tk, tq), 1) // MLA_CHUNK
    diag_bias = jnp.where(key_c <= qry_c, 0.0, NEG_BIG)
    for i in range(n_diag):
        step(n_full + i, i * tk, tq, diag_bias[:, :tq - i * tk])
    a0, a1 = a_sc[0], a_sc[1]
    o_t = jnp.concatenate([a0[:v_dim] / a0[v_dim:v_dim + 1], a1[v_dim:] / a1[0:1]], axis=0)
    o_ref[0] = o_t.T.astype(BF16)


def _mla_attn(q, k, vt, *, v_dim):
    b, t, hw = q.shape
    n_pair = hw // (2 * LANES)
    tk = vt.shape[4]
    tq = min(ATTN_Q_TILE, t)
    assert 2 * v_dim == LANES and vt.shape == (b, n_pair, t // tk, 2 * LANES, tk)
    assert t % tq == 0 and tq % tk == 0 and tk % MLA_CHUNK == 0
    kern = functools.partial(_mla_attn_kernel, tq=tq, tk=tk, v_dim=v_dim)
    return pl.pallas_call(
        kern,
        grid=(b, n_pair, t // tq),
        in_specs=[pl.BlockSpec((1, tq, 2 * LANES), lambda bi, j, i: (bi, i, j)),
                  pl.BlockSpec((1, t, 2 * LANES), lambda bi, j, i: (bi, 0, j)),
                  pl.BlockSpec((1, 1) + vt.shape[2:], lambda bi, j, i: (bi, j, 0, 0, 0))],
        out_specs=pl.BlockSpec((1, tq, LANES), lambda bi, j, i: (bi, i, j)),
        out_shape=jax.ShapeDtypeStruct((b, t, n_pair * LANES), BF16),
        scratch_shapes=[pltpu.VMEM((2, SUBLANES, tq), F32), pltpu.VMEM((2, LANES, tq), F32)],
        compiler_params=_params(("parallel", "parallel", "arbitrary")),
        name="mla_prompt_attn",
    )(q, k, vt)


def _dec_attn_kernel(q_ref, cc_ref, kc_ref, cn_ref, kn_ref, wuk_ref, wuv_ref, o_ref, qa_sc, qr_sc,
                     *, n_heads, ts, past, nope, rope, v_dim):
    for hh in range(n_heads):
        qh = q_ref[0, :, hh * LANES:(hh + 1) * LANES]
        qa_sc[hh * ts:(hh + 1) * ts, :] = _dot(qh[:, :nope], wuk_ref[hh]).astype(BF16)
        qr_sc[hh * ts:(hh + 1) * ts, :] = qh[:, nope:nope + rope]
    qa, qr = qa_sc[...], qr_sc[...]
    cc = cc_ref[0].astype(BF16)
    cn = cn_ref[0].astype(BF16)
    s1 = _nt_dot(qa, cc) + _dot(qr, kc_ref[0].astype(BF16))
    s2 = _nt_dot(qa, cn) + _nt_dot(qr, kn_ref[0].astype(BF16))
    rows = n_heads * ts
    qchunk1 = (past + lax.broadcasted_iota(jnp.int32, (rows, past), 0) % ts) // MLA_CHUNK
    kchunk1 = lax.broadcasted_iota(jnp.int32, (rows, past), 1) // MLA_CHUNK
    s1 = jnp.where(kchunk1 <= qchunk1, s1, NEG_BIG)
    qchunk2 = (past + lax.broadcasted_iota(jnp.int32, (rows, ts), 0) % ts) // MLA_CHUNK
    kchunk2 = (past + lax.broadcasted_iota(jnp.int32, (rows, ts), 1)) // MLA_CHUNK
    s2 = jnp.where(kchunk2 <= qchunk2, s2, NEG_BIG)
    m = jnp.maximum(jnp.max(s1, axis=-1, keepdims=True), jnp.max(s2, axis=-1, keepdims=True))
    p1 = jnp.exp2(s1 - m)
    p2 = jnp.exp2(s2 - m)
    l = jnp.sum(p1, axis=-1, keepdims=True) + jnp.sum(p2, axis=-1, keepdims=True)
    lat = ((_dot(p1.astype(BF16), cc) + _dot(p2.astype(BF16), cn)) / l).astype(BF16)
    per_slab = LANES // v_dim
    for j in range(n_heads // per_slab):
        parts = [_dot(lat[(j * per_slab + i) * ts:(j * per_slab + i + 1) * ts, :], wuv_ref[j * per_slab + i])
                 for i in range(per_slab)]
        o_ref[0, :, j * LANES:(j + 1) * LANES] = jnp.concatenate(parts, axis=-1).astype(BF16)


def _dec_attn(q, ckv_cache, kr_cache, ckv_new, kr_new, wuk_t, wuv_h, *, nope, rope, v_dim):
    b, ts, hw = q.shape
    n_heads = hw // LANES
    past, kv_rank = ckv_cache.shape[1:]
    kern = functools.partial(_dec_attn_kernel, n_heads=n_heads, ts=ts, past=past, nope=nope, rope=rope,
                             v_dim=v_dim)
    per_b = lambda a: pl.BlockSpec((1,) + a.shape[1:], lambda bi: (bi,) + (0,) * (a.ndim - 1))
    full = lambda a: pl.BlockSpec(a.shape, lambda bi: (0,) * a.ndim)
    return pl.pallas_call(
        kern,
        grid=(b,),
        in_specs=[per_b(q), per_b(ckv_cache), per_b(kr_cache), per_b(ckv_new), per_b(kr_new),
                  full(wuk_t), full(wuv_h)],
        out_specs=pl.BlockSpec((1, ts, n_heads * v_dim), lambda bi: (bi, 0, 0)),
        out_shape=jax.ShapeDtypeStruct((b, ts, n_heads * v_dim), BF16),
        scratch_shapes=[pltpu.VMEM((n_heads * ts, kv_rank), BF16), pltpu.VMEM((n_heads * ts, rope), BF16)],
        compiler_params=_params(("parallel",)),
        name="mla_decode_attn",
    )(q, ckv_cache, kr_cache, ckv_new, kr_new, wuk_t, wuv_h)


def _ret_kernel(rq_ref, rk_ref, rv_ref, gt_ref, s0_ref, g_ref, o_ref, sfin_ref, s_sc,
                *, lc, n_sub, log_g):
    t = pl.program_id(1)

    @pl.when(t == 0)
    def _():
        s_sc[...] = s0_ref[0]

    diff = (lax.broadcasted_iota(jnp.int32, (lc, lc), 0)
            - lax.broadcasted_iota(jnp.int32, (lc, lc), 1)).astype(F32)
    pos = lax.broadcasted_iota(jnp.int32, (lc, 1), 0).astype(F32)
    for hh, lg in enumerate(log_g):
        sl = slice(hh * LANES, (hh + 1) * LANES)
        dmask = jnp.where(diff >= 0, jnp.exp(lg * jnp.maximum(diff, 0.0)), 0.0)
        q_dec = jnp.exp(lg * (pos + 1.0))
        k_dec = jnp.exp(lg * (lc - 1.0 - pos))
        s_dec = math.exp(lg * lc)
        gain = g_ref[:, sl]
        for c in range(n_sub):
            rows = slice(c * lc, (c + 1) * lc)
            q = rq_ref[0, rows, sl]
            k = rk_ref[0, rows, sl]
            v = rv_ref[0, rows, sl]
            s_prev = s_sc[hh]
            inner = _nt_dot(q, k) * dmask
            o = _dot(inner.astype(BF16), v)
            o = o + _dot((q.astype(F32) * q_dec).astype(BF16), s_prev.astype(BF16))
            kd = (k.astype(F32) * k_dec).astype(BF16)
            s_sc[hh] = s_dec * s_prev + lax.dot_general(kd, v, (((0,), (0,)), ((), ())),
                                                        preferred_element_type=F32)
            mu = jnp.mean(o, axis=-1, keepdims=True)
            oc = o - mu
            var = jnp.mean(oc * oc, axis=-1, keepdims=True)
            y = oc * lax.rsqrt(var + EPS) * gain * gt_ref[0, rows, sl].astype(F32)
            o_ref[0, rows, sl] = y.astype(BF16)

    @pl.when(t == pl.num_programs(1) - 1)
    def _():
        sfin_ref[0] = s_sc[...]


def _retention(rq, rk, rv, gate, s0, gn_g, log_g):
    b, t, w = rq.shape
    n_heads, dk, dv = s0.shape[1:]
    lb = min(RET_BLOCK, t)
    lc = min(RET_CHUNK, lb)
    assert t % lb == 0 and lb % lc == 0 and dk == LANES and dv == LANES
    kern = functools.partial(_ret_kernel, lc=lc, n_sub=lb // lc, log_g=log_g)
    blk = pl.BlockSpec((1, lb, w), lambda bi, ti: (bi, ti, 0))
    st = pl.BlockSpec((1, n_heads, dk, dv), lambda bi, ti: (bi, 0, 0, 0))
    return pl.pallas_call(
        kern,
        grid=(b, t // lb),
        in_specs=[blk, blk, blk, blk, st, pl.BlockSpec(gn_g.shape, lambda bi, ti: (0, 0))],
        out_specs=(blk, st),
        out_shape=(jax.ShapeDtypeStruct((b, t, w), BF16), jax.ShapeDtypeStruct(s0.shape, F32)),
        scratch_shapes=[pltpu.VMEM((n_heads, dk, dv), F32)],
        compiler_params=_params(("parallel", "arbitrary")),
        name="retention",
    )(rq, rk, rv, gate, s0, gn_g)


def _memkv_kernel(m_ref, g_ref, w_ref, k_out, v_out):
    h = _rms(m_ref[...], g_ref[...]).astype(BF16)
    n = k_out.shape[1]
    k_out[...] = _dot(h, w_ref[:, :n])
    v_out[...] = _dot(h, w_ref[:, n:])


def _memkv(mem2d, g, wkv):
    rows, d = mem2d.shape
    n = wkv.shape[1] // 2
    tm = min(ROW_TILE, rows)
    assert rows % tm == 0
    row = lambda w: pl.BlockSpec((tm, w), lambda i: (i, 0))
    full = lambda a: pl.BlockSpec(a.shape, lambda i: (0,) * a.ndim)
    return pl.pallas_call(
        _memkv_kernel,
        grid=(rows // tm,),
        in_specs=[row(d), full(g), full(wkv)],
        out_specs=(row(n), row(n)),
        out_shape=(jax.ShapeDtypeStruct((rows, n), F32), jax.ShapeDtypeStruct((rows, n), F32)),
        compiler_params=_params(("parallel",)),
        name="mem_kv",
    )(mem2d, g, wkv)


def _mix_kernel(x_ref, om_ref, or_ref, wo_ref, g_ref, wcq_ref, wco_ref, mk_ref, mv_ref, o_ref, qc_sc, oc_sc,
                *, bb, tt, n_heads, dh):
    mixed = jnp.concatenate([om_ref[...], or_ref[...]], axis=-1)
    x1 = x_ref[...] + _dot(mixed, wo_ref[...])
    h = _rms(x1, g_ref[...]).astype(BF16)
    qc_sc[...] = (_dot(h, wcq_ref[...]) * (float(dh) ** -0.5)).astype(BF16)
    for bi in range(bb):
        rows = slice(bi * tt, (bi + 1) * tt)
        for hh in range(n_heads):
            sl = slice(hh * dh, (hh + 1) * dh)
            s = _nt_dot(qc_sc[rows, sl], mk_ref[bi, :, sl].astype(BF16))
            p = jnp.exp(s - jnp.max(s, axis=-1, keepdims=True))
            l = jnp.sum(p, axis=-1, keepdims=True)
            o = _dot(p.astype(BF16), mv_ref[bi, :, sl].astype(BF16)) / l
            oc_sc[rows, sl] = o.astype(BF16)
    o_ref[...] = x1 + _dot(oc_sc[...], wco_ref[...])


def _mix(x2d, om, orr, wo, g, wcq, mk, mv, wco, *, bb, tt, n_heads):
    rows, d = x2d.shape
    b = mk.shape[0]
    r = bb * tt
    n_t = rows // (b * tt)
    assert rows % r == 0 and b % bb == 0 and (bb == 1 or n_t == 1) and mk.shape == mv.shape
    row = lambda w: pl.BlockSpec((r, w), lambda gi, ti: (gi * n_t + ti, 0))
    full = lambda a: pl.BlockSpec(a.shape, lambda gi, ti: (0,) * a.ndim)
    mem = pl.BlockSpec((bb,) + mk.shape[1:], lambda gi, ti: (gi,) + (0,) * (mk.ndim - 1))
    kern = functools.partial(_mix_kernel, bb=bb, tt=tt, n_heads=n_heads, dh=d // n_heads)
    return pl.pallas_call(
        kern,
        grid=(b // bb, n_t),
        in_specs=[row(d), row(om.shape[1]), row(orr.shape[1]), full(wo), full(g), full(wcq), full(wco),
                  mem, mem],
        out_specs=row(d),
        out_shape=jax.ShapeDtypeStruct((rows, d), F32),
        scratch_shapes=[pltpu.VMEM((r, d), BF16), pltpu.VMEM((r, d), BF16)],
        compiler_params=_params(("parallel", "arbitrary")),
        name="mix_out_mem_attn",
    )(x2d, om, orr, wo, g, wcq, wco, mk, mv)


def _ffn_kernel(x_ref, g_ref, wup_ref, cw_ref, cb_ref, wdn_ref, st_ref, gf_ref, y_ref, st_out, carry,
                ext_a, ext_g, *, bb, tt, ff, fc):
    t = pl.program_id(1)

    @pl.when(t == 0)
    def _():
        carry[...] = st_ref[...]

    r = bb * tt
    x = x_ref[...]
    h = _rms(x, g_ref[...]).astype(BF16)
    halo = ext_a.shape[1] - tt

    def conv(cs, ext):
        u = _dot(h, wup_ref[:, cs])
        ext[:, halo:, :] = u.reshape(bb, tt, fc)
        ext[:, halo - 2:halo, :] = carry[:, :, cs]
        um2 = ext[:, halo - 2:halo - 2 + tt, :].reshape(r, fc)
        um1 = ext[:, halo - 1:halo - 1 + tt, :].reshape(r, fc)
        carry[:, :, cs] = ext[:, halo + tt - 2:, :]
        return cb_ref[:, cs] + cw_ref[0:1, cs] * um2 + cw_ref[1:2, cs] * um1 + cw_ref[2:3, cs] * u

    acc = jnp.zeros(x.shape, F32)
    for c in range(ff // fc):
        a = conv(slice(c * fc, (c + 1) * fc), ext_a)
        gte = conv(slice(ff + c * fc, ff + (c + 1) * fc), ext_g)
        acc = acc + _dot((_silu(a) * gte).astype(BF16), wdn_ref[c * fc:(c + 1) * fc, :])
    y_ref[...] = _rms(x + acc, gf_ref[...])
    st_out[...] = carry[...]


SEG = SUBLANES * SUBLANES


def _ffn_seq_kernel(x_ref, g_ref, wup_ref, cw_ref, cb_ref, wdn_ref, st_ref, gf_ref, y_ref, st_out, carry,
                    slab, act, *, tt, ff, fc):
    t = pl.program_id(1)
    d = x_ref.shape[1]
    n_seg, n_slab = tt // SEG, d // LANES

    @pl.when(t == 0)
    def _():
        for j in range(2):
            carry[j] = jnp.broadcast_to(st_ref[0, j:j + 1, :], carry.shape[1:])

    for k in range(n_slab):
        for c in range(n_seg):
            for a in range(SUBLANES):
                src = x_ref[SEG * c + SUBLANES * a:SEG * c + SUBLANES * (a + 1), k * LANES:(k + 1) * LANES]
                slab[k, pl.ds(SEG * c + a, SUBLANES, stride=SUBLANES), :] = src
    x = jnp.concatenate([slab[k] for k in range(n_slab)], axis=-1)
    h = _rms(x, g_ref[...]).astype(BF16)
    sub0 = lax.broadcasted_iota(jnp.int32, (n_seg, SUBLANES, fc), 1) == 0

    def conv(cs):
        u = _dot(h, wup_ref[:, cs]).reshape(n_seg, SUBLANES, SUBLANES, fc)

        def shifted(j):
            prev = jnp.concatenate([carry[j - 6, :, cs][None], u[:-1, j]], axis=0)
            return jnp.where(sub0, pltpu.roll(prev, 1, 1), pltpu.roll(u[:, j], 1, 1))[:, None]

        r6, r7 = shifted(6), shifted(7)
        tap1 = jnp.concatenate([r7, u[:, :SUBLANES - 1]], axis=1)
        tap2 = jnp.concatenate([r6, r7, u[:, :SUBLANES - 2]], axis=1)
        carry[0, :, cs] = u[n_seg - 1, 6]
        carry[1, :, cs] = u[n_seg - 1, 7]
        cv = cb_ref[:, cs] + cw_ref[0:1, cs] * tap2 + cw_ref[1:2, cs] * tap1 + cw_ref[2:3, cs] * u
        return cv.reshape(tt, fc)

    for c in range(ff // fc):
        a = conv(slice(c * fc, (c + 1) * fc))
        gte = conv(slice(ff + c * fc, ff + (c + 1) * fc))
        act[:, c * fc:(c + 1) * fc] = (_silu(a) * gte).astype(BF16)
    y = _rms(x + _dot(act[...], wdn_ref[...]), gf_ref[...])

    for k in range(n_slab):
        slab[k] = y[:, k * LANES:(k + 1) * LANES]
    for k in range(n_slab):
        for c in range(n_seg):
            for a in range(SUBLANES):
                y_ref[SEG * c + SUBLANES * a:SEG * c + SUBLANES * (a + 1), k * LANES:(k + 1) * LANES] = (
                    slab[k, pl.ds(SEG * c + a, SUBLANES, stride=SUBLANES), :])
    for j in range(2):
        st_out[0, j:j + 1, :] = carry[j, SUBLANES - 1:SUBLANES, :]


def _ffn(x2d, g, wup, cw, cb, wdn, state, gfin, *, bb, tt):
    rows, d = x2d.shape
    b, n_keep, ff2 = state.shape
    assert n_keep == 2 and tt >= 2
    ff = ff2 // 2
    fc = FF_CHUNK
    assert ff % fc == 0
    r = bb * tt
    n_t = rows // (b * tt)
    assert rows % r == 0 and b % bb == 0 and (bb == 1 or n_t == 1)
    row = pl.BlockSpec((r, d), lambda gi, ti: (gi * n_t + ti, 0))
    full = lambda a: pl.BlockSpec(a.shape, lambda gi, ti: (0,) * a.ndim)
    st = pl.BlockSpec((bb, n_keep, ff2), lambda gi, ti: (gi, 0, 0))
    if bb == 1 and tt % SEG == 0 and d % LANES == 0:
        kern = functools.partial(_ffn_seq_kernel, tt=tt, ff=ff, fc=fc)
        scratch = [pltpu.VMEM((n_keep, SUBLANES, ff2), F32), pltpu.VMEM((d // LANES, tt, LANES), F32),
                   pltpu.VMEM((tt, ff), BF16)]
    else:
        kern = functools.partial(_ffn_kernel, bb=bb, tt=tt, ff=ff, fc=fc)
        scratch = [pltpu.VMEM((bb, n_keep, ff2), F32),
                   pltpu.VMEM((bb, SUBLANES + tt, fc), F32), pltpu.VMEM((bb, SUBLANES + tt, fc), F32)]
    return pl.pallas_call(
        kern,
        grid=(b // bb, n_t),
        in_specs=[row, full(g), full(wup), full(cw), full(cb), full(wdn), st, full(gfin)],
        out_specs=(row, st),
        out_shape=(jax.ShapeDtypeStruct((rows, d), F32), jax.ShapeDtypeStruct(state.shape, F32)),
        scratch_shapes=scratch,
        compiler_params=_params(("parallel", "arbitrary")),
        name="conv_ffn",
    )(x2d, g, wup, cw, cb, wdn, state, gfin)


def _rope_tables(base_pos, off_pos, rope):
    nope = LANES - 2 * rope
    mf = 1.0 / (ROPE_BASE ** (jnp.arange(0, rope, 2, dtype=F32) / rope))
    f_mla = jnp.concatenate([jnp.zeros((nope,), F32), mf, mf, jnp.zeros((rope,), F32)])
    rf = 1.0 / (ROPE_BASE ** jnp.linspace(0.0, 1.0, LANES // 2, dtype=F32))
    f_ret = jnp.concatenate([rf, rf])
    sign = jnp.concatenate([-jnp.ones((LANES // 2,), F32), jnp.ones((LANES // 2,), F32)])

    def tab(pos):
        p = pos.astype(F32)[:, None]
        return jnp.stack([jnp.cos(p * f_mla), jnp.sin(p * f_mla), jnp.cos(p * f_ret), sign * jnp.sin(p * f_ret)])

    return jnp.transpose(tab(base_pos), (1, 0, 2)), tab(off_pos)


def _layer_weights(w_in, w_uq, w_uk, w_uv, q_rank, kv_rank, rope, n_ret):
    d = w_in.shape[0]
    n_mla, qk = w_uq.shape[1:]
    nope = qk - rope
    v_dim = w_uv.shape[2]
    half = rope // 2
    assert nope + 2 * rope == LANES and kv_rank == LANES and w_uk.shape[2] == nope
    ret_w = n_ret * LANES
    o_kv, o_kr = q_rank, q_rank + kv_rank
    o_rq = o_kr + rope
    assert w_in.shape[1] == o_rq + 4 * ret_w
    wkr = w_in[:, o_kr:o_rq]
    wkr_rot = jnp.concatenate([-wkr[:, half:], wkr[:, :half]], axis=1)
    zl, zr = jnp.zeros((d, nope), w_in.dtype), jnp.zeros((d, rope), w_in.dtype)
    w1 = jnp.concatenate([w_in[:, :o_kr], zl, wkr, zr, zl, wkr_rot, zr, w_in[:, o_rq:]], axis=1).astype(BF16)
    c0 = o_kr
    cols = (0, o_kv, c0, c0 + LANES, c0 + 2 * LANES, c0 + 2 * LANES + ret_w, c0 + 2 * LANES + 2 * ret_w,
            c0 + 2 * LANES + 3 * ret_w, c0 + 2 * LANES + 4 * ret_w)

    zq = lambda w: jnp.zeros((q_rank, n_mla, w), w_uq.dtype)
    r1, r2 = w_uq[..., nope:nope + half], w_uq[..., nope + half:]
    wq = jnp.concatenate([w_uq, zq(rope)], axis=-1).reshape(q_rank, n_mla * LANES)
    wq_rot = jnp.concatenate([zq(nope), -r2, r1, zq(rope)], axis=-1).reshape(q_rank, n_mla * LANES)
    wqq = jnp.concatenate([wq, wq_rot], axis=1).astype(BF16)

    wk_nope = jnp.concatenate([w_uk, jnp.zeros((kv_rank, n_mla, LANES - nope), w_uk.dtype)], axis=-1)
    lane = jnp.arange(LANES)
    live = (lane >= nope) & (lane < nope + rope)
    place = jnp.where(live[:, None], jnp.eye(LANES, dtype=F32), 0.0)
    wk_rope = jnp.broadcast_to(place[:, None, :], (LANES, n_mla, LANES))
    wk = jnp.concatenate([wk_nope, wk_rope], axis=0).reshape(kv_rank + LANES, n_mla * LANES).astype(BF16)
    assert n_mla % 2 == 0 and 2 * v_dim == LANES
    zv = jnp.zeros((kv_rank, n_mla // 2, v_dim), w_uv.dtype)
    wv = jnp.stack([jnp.concatenate([w_uv[:, 0::2], zv], axis=-1),
                    jnp.concatenate([zv, w_uv[:, 1::2]], axis=-1)], axis=2)
    wv = wv.reshape(kv_rank, n_mla * LANES).astype(BF16)
    wuk_t = jnp.transpose(w_uk, (1, 2, 0)).astype(BF16)
    wuv_h = jnp.transpose(w_uv, (1, 0, 2)).astype(BF16)
    return w1, cols, wqq, wk, wv, wuk_t, wuv_h, nope, v_dim, n_mla


def kernel(x_prompt, x_sample, cache_mla_ckv, cache_mla_krope, state_ret, state_ffn_conv, cache_mem_k,
           cache_mem_v, mem_prompt, norm_mix_g, w_in, q_norm_g, kv_norm_g, w_uq, w_uk, w_uv, ret_gn_g, w_o,
           norm_mem_g, mem_norm_g, w_cq, w_ck, w_cv, w_co, norm_ffn_g, w_up, conv_w, conv_b, w_down,
           final_norm_g):
    bp, tp, d = x_prompt.shape
    bs, ts, _ = x_sample.shape
    depth = w_in.shape[0]
    past = cache_mla_ckv.shape[2]
    q_rank, kv_rank = w_uq.shape[1], w_uk.shape[1]
    rope = cache_mla_krope.shape[3]
    n_ret, ret_dk, ret_dv = state_ret.shape[2:]
    mem_tok, mem_heads, mem_dh = cache_mem_k.shape[2:]
    ff2 = w_up.shape[2]
    assert conv_w.shape[1] == 3 and ret_dk == LANES
    qk = w_uq.shape[3]
    q_scale = float(qk) ** -0.5 * LOG2E
    log_g = tuple(math.log(1.0 - 2.0 ** (-5.0 - i)) for i in range(n_ret))

    tile_p = min(ROW_TILE, tp)
    tile_s = min(ROW_TILE, bs * ts)
    assert tp % tile_p == 0 and tile_s % ts == 0
    tabs_p = _rope_tables(jnp.arange(0, tp, tile_p), jnp.arange(tile_p), rope)
    tabs_s = _rope_tables(jnp.full((1,), past), jnp.arange(tile_s) % ts, rope)

    hp = x_prompt.reshape(bp * tp, d)
    hs = x_sample.reshape(bs * ts, d)
    row2 = lambda a: a.reshape(1, -1)
    outs = {k: [] for k in ("p_ckv", "p_kr", "p_ret", "p_conv", "p_mk", "p_mv", "s_ckv", "s_kr", "s_ret", "s_conv")}
    for l in range(depth):
        w1, cols, wqq, wk, wv, wuk_t, wuv_h, nope, v_dim, n_mla = _layer_weights(
            w_in[l], w_uq[l], w_uk[l], w_uv[l], q_rank, kv_rank, rope, n_ret)
        inproj = functools.partial(
            _inproj, g=row2(norm_mix_g[l]), w1=w1, qg=row2(q_norm_g[l]), kvg=row2(kv_norm_g[l]), wqq=wqq,
            wk=wk, wv=wv, cols=cols, n_mla=n_mla, n_ret=n_ret, nope=nope, rope=rope, kv_rank=kv_rank,
            v_dim=v_dim, q_scale=q_scale)
        wo = w_o[l].astype(BF16)
        wcq = w_cq[l].reshape(d, mem_heads * mem_dh).astype(BF16)
        wco = w_co[l].astype(BF16)
        wckv = jnp.concatenate([w_ck[l].reshape(d, -1), w_cv[l].reshape(d, -1)], axis=1).astype(BF16)
        wup = w_up[l].astype(BF16)
        wdn = w_down[l].astype(BF16)
        gn = row2(ret_gn_g[l])
        mix = functools.partial(_mix, wo=wo, g=row2(norm_mem_g[l]), wcq=wcq, wco=wco, n_heads=mem_heads)
        ffn = functools.partial(_ffn, g=row2(norm_ffn_g[l]), wup=wup, cw=conv_w[l], cb=row2(conv_b[l]),
                                wdn=wdn, gfin=row2(final_norm_g))

        q, k, v, ckv, kr, rq, rk, rv, rg = inproj(hp, tabs=tabs_p)
        o_mla = _mla_attn(q.reshape(bp, tp, -1), k.reshape(bp, tp, -1), v, v_dim=v_dim)
        b3 = lambda a: a.reshape(bp, tp, -1)
        o_ret, ret_fin = _retention(b3(rq), b3(rk), b3(rv), b3(rg),
                                    jnp.zeros((bp, n_ret, ret_dk, ret_dv), F32), gn, log_g)
        mk, mv = _memkv(mem_prompt.reshape(bp * mem_tok, d), row2(mem_norm_g[l]), wckv)
        x2 = mix(hp, o_mla.reshape(bp * tp, -1), o_ret.reshape(bp * tp, -1),
                 mk=mk.reshape(bp, mem_tok, -1), mv=mv.reshape(bp, mem_tok, -1), bb=1, tt=tile_p)
        hp_next, conv_fin = ffn(x2, state=jnp.zeros((bp, 2, ff2), F32), bb=1, tt=math.gcd(FFN_TILE, tp))
        outs["p_ckv"].append(ckv.reshape(bp, tp, kv_rank))
        outs["p_kr"].append(kr.reshape(bp, tp, rope))
        outs["p_ret"].append(ret_fin)
        outs["p_conv"].append(conv_fin)
        outs["p_mk"].append(mk.reshape(bp, mem_tok, mem_heads, mem_dh))
        outs["p_mv"].append(mv.reshape(bp, mem_tok, mem_heads, mem_dh))

        q, k, v, ckv, kr, rq, rk, rv, rg = inproj(hs, tabs=tabs_s)
        o_mla = _dec_attn(q.reshape(bs, ts, -1), cache_mla_ckv[l], jnp.swapaxes(cache_mla_krope[l], 1, 2),
                          ckv.reshape(bs, ts, -1), kr.reshape(bs, ts, -1), wuk_t, wuv_h,
                          nope=nope, rope=rope, v_dim=v_dim)
        s3 = lambda a: a.reshape(bs, ts, -1)
        o_ret, ret_fin = _retention(s3(rq), s3(rk), s3(rv), s3(rg), state_ret[l], gn, log_g)
        bb = math.gcd(bs, DEC_MEM_GROUP)
        x2 = mix(hs, o_mla.reshape(bs * ts, -1), o_ret.reshape(bs * ts, -1),
                 mk=cache_mem_k[l].reshape(bs, mem_tok, -1), mv=cache_mem_v[l].reshape(bs, mem_tok, -1),
                 bb=bb, tt=ts)
        hs_next, conv_fin = ffn(x2, state=state_ffn_conv[l], bb=bs, tt=ts)
        outs["s_ckv"].append(ckv.reshape(bs, ts, kv_rank))
        outs["s_kr"].append(kr.reshape(bs, ts, rope))
        outs["s_ret"].append(ret_fin)
        outs["s_conv"].append(conv_fin)
        assert depth == 1
        hp, hs = hp_next, hs_next

    st = lambda name: jnp.stack(outs[name])
    return (hp.reshape(bp, tp, d), hs.reshape(bs, ts, d),
            st("p_ckv"), st("p_kr"), st("p_ret"), st("p_conv"), st("p_mk"), st("p_mv"),
            st("s_ckv"), st("s_kr"), st("s_ret"), st("s_conv"))
```

```python
import functools
import math

import jax
import jax.numpy as jnp
from jax import lax
from jax.experimental import pallas as pl
from jax.experimental.pallas import tpu as pltpu

F32 = jnp.float32
BF16 = jnp.bfloat16

EPS = 1e-6
ROPE_BASE = 10000.0
MLA_CHUNK = 64
LOG2E = 1.4426950408889634
LANES = 128
SUBLANES = 8
NEG_BIG = -1e30

ROW_TILE = 512
PROMPT_TILE = 1024
ATTN_Q_TILE = 2048
ATTN_K_TILE = 512
RET_BLOCK = 1024
RET_CHUNK = 256
FF_CHUNK = 256
DEC_MEM_GROUP = 8
VMEM_LIMIT = 56 * 1024 * 1024


def _nt_dot(a, b):
    return lax.dot_general(a, b, (((1,), (1,)), ((), ())), preferred_element_type=F32)


def _dot(a, b):
    return jnp.dot(a, b, preferred_element_type=F32)


def _rms(x, g):
    return x * lax.rsqrt(jnp.mean(x * x, axis=-1, keepdims=True) + EPS) * g


def _silu(x):
    return x / (1.0 + jnp.exp(-x))


def _params(sem):
    return pltpu.CompilerParams(dimension_semantics=sem, vmem_limit_bytes=VMEM_LIMIT)


def _inproj_kernel(x_ref, g_ref, w1_ref, qg_ref, kvg_ref, wqq_ref, wk_ref, wv_ref, base_ref, off_ref,
                   q_out, k_out, v_out, ckv_out, kr_out, rq_out, rk_out, rv_out, rg_out,
                   *, cols, n_mla, n_ret, nope, rope, v_dim, q_scale, rk_scale):
    o_q, o_kv, o_kr, o_rq, o_rk, o_rv, o_rg, o_end = cols
    h = _rms(x_ref[...], g_ref[...]).astype(BF16)

    ca, sa, cra, sra = (base_ref[0, i:i + 1, :] for i in range(4))
    cb, sb, crb, srb = (off_ref[i] for i in range(4))
    cq_t, sq_t = ca * cb - sa * sb, sa * cb + ca * sb
    cr_t, sr_t = cra * crb - sra * srb, sra * crb + cra * srb

    cq = _rms(_dot(h, w1_ref[:, o_q:o_kv]), qg_ref[...]).astype(BF16)
    qq = _dot(cq, wqq_ref[...])
    for hh in range(n_mla):
        a = qq[:, hh * LANES:(hh + 1) * LANES]
        b = qq[:, (n_mla + hh) * LANES:(n_mla + hh + 1) * LANES]
        slab = (a * cq_t + b * sq_t) * q_scale
        if len(q_out.shape) == 3:
            q_out[0, hh * LANES:(hh + 1) * LANES, :] = slab.T.astype(BF16)
        else:
            q_out[:, hh * LANES:(hh + 1) * LANES] = slab.astype(BF16)

    ckv = _rms(_dot(h, w1_ref[:, o_kv:o_kr]), kvg_ref[...])
    ckv_out[...] = ckv
    zkr = _dot(h, w1_ref[:, o_kr:o_rq])
    kr = zkr * cq_t + pltpu.roll(zkr, LANES - rope, 1) * sq_t
    kr_out[...] = kr[:, nope:nope + rope]
    ckv_b = ckv.astype(BF16)
    k_out[...] = _dot(jnp.concatenate([ckv_b, kr.astype(BF16)], axis=-1), wk_ref[...]).astype(BF16)
    v_nat = _dot(ckv_b, wv_ref[...])
    lane = lax.broadcasted_iota(jnp.int32, (1, LANES), 1)
    one_even, one_odd = (lane == v_dim).astype(F32), (lane == 0).astype(F32)
    slabs = []
    for j in range(n_mla // 2):
        pair = v_nat[:, j * LANES:(j + 1) * LANES]
        slabs += [jnp.where(lane < v_dim, pair, one_even), jnp.where(lane >= v_dim, pair, one_odd)]
    v_slabs = jnp.concatenate(slabs, axis=-1)
    v_t = v_slabs.T.astype(BF16)
    tk = v_out.shape[4]
    for j in range(v_out.shape[2]):
        v_out[0, :, j, :, :] = v_t[:, j * tk:(j + 1) * tk].reshape(n_mla // 2, 2 * LANES, tk)

    zq = _dot(h, w1_ref[:, o_rq:o_rk])
    zk = _dot(h, w1_ref[:, o_rk:o_rv])
    for hh in range(n_ret):
        sl = slice(hh * LANES, (hh + 1) * LANES)
        a = zq[:, sl]
        rq_out[:, sl] = (a * cr_t + pltpu.roll(a, LANES // 2, 1) * sr_t).astype(BF16)
        b = zk[:, sl]
        rk_out[:, sl] = ((b * cr_t + pltpu.roll(b, LANES // 2, 1) * sr_t) * rk_scale).astype(BF16)
    rv_out[...] = _dot(h, w1_ref[:, o_rv:o_rg]).astype(BF16)
    rg_out[...] = _silu(_dot(h, w1_ref[:, o_rg:o_end])).astype(BF16)


def _inproj(x2d, g, w1, qg, kvg, wqq, wk, wv, tabs, *, cols, n_mla, n_ret, nope, rope, kv_rank, v_dim,
            q_scale, q_transposed):
    rows, d = x2d.shape
    base_tab, off_tab = tabs
    tm = off_tab.shape[1]
    assert rows % tm == 0
    base_tiles = base_tab.shape[0]
    tk = math.gcd(ATTN_K_TILE, tm)
    ret_w = n_ret * LANES
    row = lambda w: pl.BlockSpec((tm, w), lambda i: (i, 0))
    full = lambda a: pl.BlockSpec(a.shape, lambda i: (0,) * a.ndim)
    base = pl.BlockSpec((1,) + base_tab.shape[1:], lambda i: (i % base_tiles, 0, 0))
    n_seq = rows // (tm * base_tiles)
    out_shape = (
        jax.ShapeDtypeStruct((n_seq, n_mla * LANES, tm * base_tiles) if q_transposed
                             else (rows, n_mla * LANES), BF16),
        jax.ShapeDtypeStruct((rows, n_mla * LANES), BF16),
        jax.ShapeDtypeStruct((rows // (tm * base_tiles), n_mla // 2, base_tiles * (tm // tk), 2 * LANES, tk),
                             BF16),
        jax.ShapeDtypeStruct((rows, kv_rank), F32),
        jax.ShapeDtypeStruct((rows, rope), F32),
        jax.ShapeDtypeStruct((rows, ret_w), BF16),
        jax.ShapeDtypeStruct((rows, ret_w), BF16),
        jax.ShapeDtypeStruct((rows, ret_w), BF16),
        jax.ShapeDtypeStruct((rows, ret_w), BF16),
    )
    kern = functools.partial(_inproj_kernel, cols=cols, n_mla=n_mla, n_ret=n_ret, nope=nope, rope=rope,
                             v_dim=v_dim, q_scale=q_scale, rk_scale=float(LANES) ** -0.5)
    return pl.pallas_call(
        kern,
        grid=(rows // tm,),
        in_specs=[row(d), full(g), full(w1), full(qg), full(kvg), full(wqq), full(wk), full(wv), base,
                  full(off_tab)],
        out_specs=tuple(
            row(s.shape[1]) if len(s.shape) == 2 else
            pl.BlockSpec((1, s.shape[1], tm), lambda i: (i // base_tiles, 0, i % base_tiles))
            if len(s.shape) == 3 else
            pl.BlockSpec((1, s.shape[1], tm // tk) + s.shape[3:],
                         lambda i: (i // base_tiles, 0, i % base_tiles, 0, 0))
            for s in out_shape),
        out_shape=out_shape,
        compiler_params=_params(("parallel",)),
        name="inproj",
    )(x2d, g, w1, qg, kvg, wqq, wk, wv, base_tab, off_tab)


def _mla_attn_kernel(qt_ref, k_ref, vt_ref, o_ref, m_sc, a_sc, *, tq, tk, v_dim):
    qi = pl.program_id(2)
    m_sc[...] = jnp.full(m_sc.shape, -jnp.inf, F32)
    a_sc[...] = jnp.zeros(a_sc.shape, F32)

    def step(kt, c0, c1, bias):
        ks = pl.multiple_of(kt * tk, tk)
        for hh in range(2):
            sl = slice(hh * LANES, (hh + 1) * LANES)
            s = _dot(k_ref[0, pl.ds(ks, tk), sl], qt_ref[0, sl, c0:c1])
            if bias is not None:
                s = s + bias
            m_old = m_sc[hh, :, c0:c1]
            m_new = jnp.maximum(m_old, jnp.max(s, axis=0, keepdims=True))
            p = jnp.exp2(s - jnp.tile(m_new, (tk // SUBLANES, 1)))
            alpha = jnp.tile(jnp.exp2(m_old - m_new), (LANES // SUBLANES, 1))
            a_sc[hh, :, c0:c1] = alpha * a_sc[hh, :, c0:c1] + _dot(vt_ref[0, 0, kt, sl, :], p.astype(BF16))
            m_sc[hh, :, c0:c1] = m_new

    n_diag = tq // tk
    n_full = qi * n_diag

    def body(j, c):
        for i in range(n_diag):
            step(j * n_diag + i, 0, tq, None)
        return c

    lax.fori_loop(0, qi, body, 0)
    key_c = lax.broadcasted_iota(jnp.int32, (tk, tq), 0) // MLA_CHUNK
    qry_c = lax.broadcasted_iota(jnp.int32, (tk, tq), 1) // MLA_CHUNK
    diag_bias = jnp.where(key_c <= qry_c, 0.0, NEG_BIG)
    for i in range(n_diag):
        step(n_full + i, i * tk, tq, diag_bias[:, :tq - i * tk])
    a0, a1 = a_sc[0], a_sc[1]
    o_t = jnp.concatenate([a0[:v_dim] / a0[v_dim:v_dim + 1], a1[v_dim:] / a1[0:1]], axis=0)
    o_ref[0] = o_t.T.astype(BF16)


def _mla_attn(qt, k, vt, *, v_dim):
    b, t, hw = k.shape
    assert qt.shape == (b, hw, t)
    n_pair = hw // (2 * LANES)
    tk = vt.shape[4]
    tq = min(ATTN_Q_TILE, t)
    assert 2 * v_dim == LANES and vt.shape == (b, n_pair, t // tk, 2 * LANES, tk)
    assert t % tq == 0 and tq % tk == 0 and tk % MLA_CHUNK == 0
    kern = functools.partial(_mla_attn_kernel, tq=tq, tk=tk, v_dim=v_dim)
    return pl.pallas_call(
        kern,
        grid=(b, n_pair, t // tq),
        in_specs=[pl.BlockSpec((1, 2 * LANES, tq), lambda bi, j, i: (bi, j, i)),
                  pl.BlockSpec((1, t, 2 * LANES), lambda bi, j, i: (bi, 0, j)),
                  pl.BlockSpec((1, 1) + vt.shape[2:], lambda bi, j, i: (bi, j, 0, 0, 0))],
        out_specs=pl.BlockSpec((1, tq, LANES), lambda bi, j, i: (bi, i, j)),
        out_shape=jax.ShapeDtypeStruct((b, t, n_pair * LANES), BF16),
        scratch_shapes=[pltpu.VMEM((2, SUBLANES, tq), F32), pltpu.VMEM((2, LANES, tq), F32)],
        compiler_params=_params(("parallel", "parallel", "arbitrary")),
        name="mla_prompt_attn",
    )(qt, k, vt)


def _dec_attn_kernel(q_ref, cc_ref, kc_ref, cn_ref, kn_ref, wuk_ref, wuv_ref, o_ref, qa_sc, qr_sc,
                     *, n_heads, ts, past, nope, rope, v_dim):
    for hh in range(n_heads):
        qh = q_ref[0, :, hh * LANES:(hh + 1) * LANES]
        qa_sc[hh * ts:(hh + 1) * ts, :] = _dot(qh[:, :nope], wuk_ref[hh]).astype(BF16)
        qr_sc[hh * ts:(hh + 1) * ts, :] = qh[:, nope:nope + rope]
    qa, qr = qa_sc[...], qr_sc[...]
    cc = cc_ref[0].astype(BF16)
    cn = cn_ref[0].astype(BF16)
    s1 = _nt_dot(qa, cc) + _dot(qr, kc_ref[0].astype(BF16))
    s2 = _nt_dot(qa, cn) + _nt_dot(qr, kn_ref[0].astype(BF16))
    rows = n_heads * ts
    qchunk1 = (past + lax.broadcasted_iota(jnp.int32, (rows, past), 0) % ts) // MLA_CHUNK
    kchunk1 = lax.broadcasted_iota(jnp.int32, (rows, past), 1) // MLA_CHUNK
    s1 = jnp.where(kchunk1 <= qchunk1, s1, NEG_BIG)
    qchunk2 = (past + lax.broadcasted_iota(jnp.int32, (rows, ts), 0) % ts) // MLA_CHUNK
    kchunk2 = (past + lax.broadcasted_iota(jnp.int32, (rows, ts), 1)) // MLA_CHUNK
    s2 = jnp.where(kchunk2 <= qchunk2, s2, NEG_BIG)
    m = jnp.maximum(jnp.max(s1, axis=-1, keepdims=True), jnp.max(s2, axis=-1, keepdims=True))
    p1 = jnp.exp2(s1 - m)
    p2 = jnp.exp2(s2 - m)
    l = jnp.sum(p1, axis=-1, keepdims=True) + jnp.sum(p2, axis=-1, keepdims=True)
    lat = ((_dot(p1.astype(BF16), cc) + _dot(p2.astype(BF16), cn)) / l).astype(BF16)
    per_slab = LANES // v_dim
    for j in range(n_heads // per_slab):
        parts = [_dot(lat[(j * per_slab + i) * ts:(j * per_slab + i + 1) * ts, :], wuv_ref[j * per_slab + i])
                 for i in range(per_slab)]
        o_ref[0, :, j * LANES:(j + 1) * LANES] = jnp.concatenate(parts, axis=-1).astype(BF16)


def _dec_attn(q, ckv_cache, kr_cache, ckv_new, kr_new, wuk_t, wuv_h, *, nope, rope, v_dim):
    b, ts, hw = q.shape
    n_heads = hw // LANES
    past, kv_rank = ckv_cache.shape[1:]
    kern = functools.partial(_dec_attn_kernel, n_heads=n_heads, ts=ts, past=past, nope=nope, rope=rope,
                             v_dim=v_dim)
    per_b = lambda a: pl.BlockSpec((1,) + a.shape[1:], lambda bi: (bi,) + (0,) * (a.ndim - 1))
    full = lambda a: pl.BlockSpec(a.shape, lambda bi: (0,) * a.ndim)
    return pl.pallas_call(
        kern,
        grid=(b,),
        in_specs=[per_b(q), per_b(ckv_cache), per_b(kr_cache), per_b(ckv_new), per_b(kr_new),
                  full(wuk_t), full(wuv_h)],
        out_specs=pl.BlockSpec((1, ts, n_heads * v_dim), lambda bi: (bi, 0, 0)),
        out_shape=jax.ShapeDtypeStruct((b, ts, n_heads * v_dim), BF16),
        scratch_shapes=[pltpu.VMEM((n_heads * ts, kv_rank), BF16), pltpu.VMEM((n_heads * ts, rope), BF16)],
        compiler_params=_params(("parallel",)),
        name="mla_decode_attn",
    )(q, ckv_cache, kr_cache, ckv_new, kr_new, wuk_t, wuv_h)


def _ret_kernel(rq_ref, rk_ref, rv_ref, gt_ref, s0_ref, g_ref, o_ref, sfin_ref, s_sc,
                *, lc, n_sub, log_g):
    t = pl.program_id(1)

    @pl.when(t == 0)
    def _():
        s_sc[...] = s0_ref[0]

    diff = (lax.broadcasted_iota(jnp.int32, (lc, lc), 0)
            - lax.broadcasted_iota(jnp.int32, (lc, lc), 1)).astype(F32)
    pos = lax.broadcasted_iota(jnp.int32, (lc, 1), 0).astype(F32)
    for hh, lg in enumerate(log_g):
        sl = slice(hh * LANES, (hh + 1) * LANES)
        dmask = jnp.where(diff >= 0, jnp.exp(lg * jnp.maximum(diff, 0.0)), 0.0)
        q_dec = jnp.exp(lg * (pos + 1.0))
        k_dec = jnp.exp(lg * (lc - 1.0 - pos))
        s_dec = math.exp(lg * lc)
        gain = g_ref[:, sl]
        for c in range(n_sub):
            rows = slice(c * lc, (c + 1) * lc)
            q = rq_ref[0, rows, sl]
            k = rk_ref[0, rows, sl]
            v = rv_ref[0, rows, sl]
            s_prev = s_sc[hh]
            inner = _nt_dot(q, k) * dmask
            o = _dot(inner.astype(BF16), v)
            o = o + _dot((q.astype(F32) * q_dec).astype(BF16), s_prev.astype(BF16))
            kd = (k.astype(F32) * k_dec).astype(BF16)
            s_sc[hh] = s_dec * s_prev + lax.dot_general(kd, v, (((0,), (0,)), ((), ())),
                                                        preferred_element_type=F32)
            mu = jnp.mean(o, axis=-1, keepdims=True)
            oc = o - mu
            var = jnp.mean(oc * oc, axis=-1, keepdims=True)
            y = oc * lax.rsqrt(var + EPS) * gain * gt_ref[0, rows, sl].astype(F32)
            o_ref[0, rows, sl] = y.astype(BF16)

    @pl.when(t == pl.num_programs(1) - 1)
    def _():
        sfin_ref[0] = s_sc[...]


def _retention(rq, rk, rv, gate, s0, gn_g, log_g):
    b, t, w = rq.shape
    n_heads, dk, dv = s0.shape[1:]
    lb = min(RET_BLOCK, t)
    lc = min(RET_CHUNK, lb)
    assert t % lb == 0 and lb % lc == 0 and dk == LANES and dv == LANES
    kern = functools.partial(_ret_kernel, lc=lc, n_sub=lb // lc, log_g=log_g)
    blk = pl.BlockSpec((1, lb, w), lambda bi, ti: (bi, ti, 0))
    st = pl.BlockSpec((1, n_heads, dk, dv), lambda bi, ti: (bi, 0, 0, 0))
    return pl.pallas_call(
        kern,
        grid=(b, t // lb),
        in_specs=[blk, blk, blk, blk, st, pl.BlockSpec(gn_g.shape, lambda bi, ti: (0, 0))],
        out_specs=(blk, st),
        out_shape=(jax.ShapeDtypeStruct((b, t, w), BF16), jax.ShapeDtypeStruct(s0.shape, F32)),
        scratch_shapes=[pltpu.VMEM((n_heads, dk, dv), F32)],
        compiler_params=_params(("parallel", "arbitrary")),
        name="retention",
    )(rq, rk, rv, gate, s0, gn_g)


def _memkv_kernel(m_ref, g_ref, w_ref, k_out, v_out):
    h = _rms(m_ref[...], g_ref[...]).astype(BF16)
    n = k_out.shape[1]
    k_out[...] = _dot(h, w_ref[:, :n])
    v_out[...] = _dot(h, w_ref[:, n:])


def _memkv(mem2d, g, wkv):
    rows, d = mem2d.shape
    n = wkv.shape[1] // 2
    tm = min(ROW_TILE, rows)
    assert rows % tm == 0
    row = lambda w: pl.BlockSpec((tm, w), lambda i: (i, 0))
    full = lambda a: pl.BlockSpec(a.shape, lambda i: (0,) * a.ndim)
    return pl.pallas_call(
        _memkv_kernel,
        grid=(rows // tm,),
        in_specs=[row(d), full(g), full(wkv)],
        out_specs=(row(n), row(n)),
        out_shape=(jax.ShapeDtypeStruct((rows, n), F32), jax.ShapeDtypeStruct((rows, n), F32)),
        compiler_params=_params(("parallel",)),
        name="mem_kv",
    )(mem2d, g, wkv)


def _mix_kernel(x_ref, om_ref, or_ref, wo_ref, g_ref, wcq_ref, wco_ref, mk_ref, mv_ref, o_ref, qc_sc, oc_sc,
                *, bb, tt, n_heads, dh):
    mixed = jnp.concatenate([om_ref[...], or_ref[...]], axis=-1)
    x1 = x_ref[...] + _dot(mixed, wo_ref[...])
    h = _rms(x1, g_ref[...]).astype(BF16)
    qc_sc[...] = (_dot(h, wcq_ref[...]) * (float(dh) ** -0.5)).astype(BF16)
    for bi in range(bb):
        rows = slice(bi * tt, (bi + 1) * tt)
        for hh in range(n_heads):
            sl = slice(hh * dh, (hh + 1) * dh)
            s = _nt_dot(qc_sc[rows, sl], mk_ref[bi, :, sl].astype(BF16))
            p = jnp.exp(s - jnp.max(s, axis=-1, keepdims=True))
            l = jnp.sum(p, axis=-1, keepdims=True)
            o = _dot(p.astype(BF16), mv_ref[bi, :, sl].astype(BF16)) / l
            oc_sc[rows, sl] = o.astype(BF16)
    o_ref[...] = x1 + _dot(oc_sc[...], wco_ref[...])


def _mix(x2d, om, orr, wo, g, wcq, mk, mv, wco, *, bb, tt, n_heads):
    rows, d = x2d.shape
    b = mk.shape[0]
    r = bb * tt
    n_t = rows // (b * tt)
    assert rows % r == 0 and b % bb == 0 and (bb == 1 or n_t == 1) and mk.shape == mv.shape
    row = lambda w: pl.BlockSpec((r, w), lambda gi, ti: (gi * n_t + ti, 0))
    full = lambda a: pl.BlockSpec(a.shape, lambda gi, ti: (0,) * a.ndim)
    mem = pl.BlockSpec((bb,) + mk.shape[1:], lambda gi, ti: (gi,) + (0,) * (mk.ndim - 1))
    kern = functools.partial(_mix_kernel, bb=bb, tt=tt, n_heads=n_heads, dh=d // n_heads)
    return pl.pallas_call(
        kern,
        grid=(b // bb, n_t),
        in_specs=[row(d), row(om.shape[1]), row(orr.shape[1]), full(wo), full(g), full(wcq), full(wco),
                  mem, mem],
        out_specs=row(d),
        out_shape=jax.ShapeDtypeStruct((rows, d), F32),
        scratch_shapes=[pltpu.VMEM((r, d), BF16), pltpu.VMEM((r, d), BF16)],
        compiler_params=_params(("parallel", "arbitrary")),
        name="mix_out_mem_attn",
    )(x2d, om, orr, wo, g, wcq, wco, mk, mv)


def _ffn_kernel(x_ref, g_ref, wup_ref, cw_ref, cb_ref, wdn_ref, st_ref, gf_ref, y_ref, st_out, carry,
                ext_a, ext_g, *, bb, tt, ff, fc):
    t = pl.program_id(1)

    @pl.when(t == 0)
    def _():
        carry[...] = st_ref[...]

    r = bb * tt
    x = x_ref[...]
    h = _rms(x, g_ref[...]).astype(BF16)
    halo = ext_a.shape[1] - tt

    def conv(cs, ext):
        u = _dot(h, wup_ref[:, cs])
        ext[:, halo:, :] = u.reshape(bb, tt, fc)
        ext[:, halo - 2:halo, :] = carry[:, :, cs]
        um2 = ext[:, halo - 2:halo - 2 + tt, :].reshape(r, fc)
        um1 = ext[:, halo - 1:halo - 1 + tt, :].reshape(r, fc)
        carry[:, :, cs] = ext[:, halo + tt - 2:, :]
        return cb_ref[:, cs] + cw_ref[0:1, cs] * um2 + cw_ref[1:2, cs] * um1 + cw_ref[2:3, cs] * u

    acc = jnp.zeros(x.shape, F32)
    for c in range(ff // fc):
        a = conv(slice(c * fc, (c + 1) * fc), ext_a)
        gte = conv(slice(ff + c * fc, ff + (c + 1) * fc), ext_g)
        acc = acc + _dot((_silu(a) * gte).astype(BF16), wdn_ref[c * fc:(c + 1) * fc, :])
    y_ref[...] = _rms(x + acc, gf_ref[...])
    st_out[...] = carry[...]


SEG = SUBLANES * SUBLANES


def _ffn_seq_kernel(x_ref, g_ref, wup_ref, cw_ref, cb_ref, wdn_ref, st_ref, gf_ref, y_ref, st_out, carry,
                    slab, act, *, tt, ff, fc):
    t = pl.program_id(1)
    d = x_ref.shape[1]
    n_seg, n_slab = tt // SEG, d // LANES

    @pl.when(t == 0)
    def _():
        for j in range(2):
            carry[j] = jnp.broadcast_to(st_ref[0, j:j + 1, :], carry.shape[1:])

    for k in range(n_slab):
        for c in range(n_seg):
            for a in range(SUBLANES):
                src = x_ref[SEG * c + SUBLANES * a:SEG * c + SUBLANES * (a + 1), k * LANES:(k + 1) * LANES]
                slab[k, pl.ds(SEG * c + a, SUBLANES, stride=SUBLANES), :] = src
    x = jnp.concatenate([slab[k] for k in range(n_slab)], axis=-1)
    h = _rms(x, g_ref[...]).astype(BF16)
    sub0 = lax.broadcasted_iota(jnp.int32, (n_seg, SUBLANES, fc), 1) == 0

    def conv(cs):
        u = _dot(h, wup_ref[:, cs]).reshape(n_seg, SUBLANES, SUBLANES, fc)

        def shifted(j):
            prev = jnp.concatenate([carry[j - 6, :, cs][None], u[:-1, j]], axis=0)
            return jnp.where(sub0, pltpu.roll(prev, 1, 1), pltpu.roll(u[:, j], 1, 1))[:, None]

        r6, r7 = shifted(6), shifted(7)
        tap1 = jnp.concatenate([r7, u[:, :SUBLANES - 1]], axis=1)
        tap2 = jnp.concatenate([r6, r7, u[:, :SUBLANES - 2]], axis=1)
        carry[0, :, cs] = u[n_seg - 1, 6]
        carry[1, :, cs] = u[n_seg - 1, 7]
        cv = cb_ref[:, cs] + cw_ref[0:1, cs] * tap2 + cw_ref[1:2, cs] * tap1 + cw_ref[2:3, cs] * u
        return cv.reshape(tt, fc)

    for c in range(ff // fc):
        a = conv(slice(c * fc, (c + 1) * fc))
        gte = conv(slice(ff + c * fc, ff + (c + 1) * fc))
        act[:, c * fc:(c + 1) * fc] = (_silu(a) * gte).astype(BF16)
    y = _rms(x + _dot(act[...], wdn_ref[...]), gf_ref[...])

    for k in range(n_slab):
        slab[k] = y[:, k * LANES:(k + 1) * LANES]
    for k in range(n_slab):
        for c in range(n_seg):
            for a in range(SUBLANES):
                y_ref[SEG * c + SUBLANES * a:SEG * c + SUBLANES * (a + 1), k * LANES:(k + 1) * LANES] = (
                    slab[k, pl.ds(SEG * c + a, SUBLANES, stride=SUBLANES), :])
    for j in range(2):
        st_out[0, j:j + 1, :] = carry[j, SUBLANES - 1:SUBLANES, :]


def _ffn(x2d, g, wup, cw, cb, wdn, state, gfin, *, bb, tt):
    rows, d = x2d.shape
    b, n_keep, ff2 = state.shape
    assert n_keep == 2 and tt >= 2
    ff = ff2 // 2
    fc = FF_CHUNK
    assert ff % fc == 0
    r = bb * tt
    n_t = rows // (b * tt)
    assert rows % r == 0 and b % bb == 0 and (bb == 1 or n_t == 1)
    row = pl.BlockSpec((r, d), lambda gi, ti: (gi * n_t + ti, 0))
    full = lambda a: pl.BlockSpec(a.shape, lambda gi, ti: (0,) * a.ndim)
    st = pl.BlockSpec((bb, n_keep, ff2), lambda gi, ti: (gi, 0, 0))
    if bb == 1 and tt % SEG == 0 and d % LANES == 0:
        kern = functools.partial(_ffn_seq_kernel, tt=tt, ff=ff, fc=fc)
        scratch = [pltpu.VMEM((n_keep, SUBLANES, ff2), F32), pltpu.VMEM((d // LANES, tt, LANES), F32),
                   pltpu.VMEM((tt, ff), BF16)]
    else:
        kern = functools.partial(_ffn_kernel, bb=bb, tt=tt, ff=ff, fc=fc)
        scratch = [pltpu.VMEM((bb, n_keep, ff2), F32),
                   pltpu.VMEM((bb, SUBLANES + tt, fc), F32), pltpu.VMEM((bb, SUBLANES + tt, fc), F32)]
    return pl.pallas_call(
        kern,
        grid=(b // bb, n_t),
        in_specs=[row, full(g), full(wup), full(cw), full(cb), full(wdn), st, full(gfin)],
        out_specs=(row, st),
        out_shape=(jax.ShapeDtypeStruct((rows, d), F32), jax.ShapeDtypeStruct(state.shape, F32)),
        scratch_shapes=scratch,
        compiler_params=_params(("parallel", "arbitrary")),
        name="conv_ffn",
    )(x2d, g, wup, cw, cb, wdn, state, gfin)


def _rope_tables(base_pos, off_pos, rope):
    nope = LANES - 2 * rope
    mf = 1.0 / (ROPE_BASE ** (jnp.arange(0, rope, 2, dtype=F32) / rope))
    f_mla = jnp.concatenate([jnp.zeros((nope,), F32), mf, mf, jnp.zeros((rope,), F32)])
    rf = 1.0 / (ROPE_BASE ** jnp.linspace(0.0, 1.0, LANES // 2, dtype=F32))
    f_ret = jnp.concatenate([rf, rf])
    sign = jnp.concatenate([-jnp.ones((LANES // 2,), F32), jnp.ones((LANES // 2,), F32)])

    def tab(pos):
        p = pos.astype(F32)[:, None]
        return jnp.stack([jnp.cos(p * f_mla), jnp.sin(p * f_mla), jnp.cos(p * f_ret), sign * jnp.sin(p * f_ret)])

    return jnp.transpose(tab(base_pos), (1, 0, 2)), tab(off_pos)


def _layer_weights(w_in, w_uq, w_uk, w_uv, q_rank, kv_rank, rope, n_ret):
    d = w_in.shape[0]
    n_mla, qk = w_uq.shape[1:]
    nope = qk - rope
    v_dim = w_uv.shape[2]
    half = rope // 2
    assert nope + 2 * rope == LANES and kv_rank == LANES and w_uk.shape[2] == nope
    ret_w = n_ret * LANES
    o_kv, o_kr = q_rank, q_rank + kv_rank
    o_rq = o_kr + rope
    assert w_in.shape[1] == o_rq + 4 * ret_w
    wkr = w_in[:, o_kr:o_rq]
    wkr_rot = jnp.concatenate([-wkr[:, half:], wkr[:, :half]], axis=1)
    zl = jnp.zeros((d, nope), w_in.dtype)
    w1 = jnp.concatenate([w_in[:, :o_kr], zl, wkr, wkr_rot, w_in[:, o_rq:]], axis=1).astype(BF16)
    c0 = o_kr + LANES
    cols = (0, o_kv, o_kr, c0, c0 + ret_w, c0 + 2 * ret_w, c0 + 3 * ret_w, c0 + 4 * ret_w)

    zq = lambda w: jnp.zeros((q_rank, n_mla, w), w_uq.dtype)
    r1, r2 = w_uq[..., nope:nope + half], w_uq[..., nope + half:]
    wq = jnp.concatenate([w_uq, zq(rope)], axis=-1).reshape(q_rank, n_mla * LANES)
    wq_rot = jnp.concatenate([zq(nope), -r2, r1, zq(rope)], axis=-1).reshape(q_rank, n_mla * LANES)
    wqq = jnp.concatenate([wq, wq_rot], axis=1).astype(BF16)

    wk_nope = jnp.concatenate([w_uk, jnp.zeros((kv_rank, n_mla, LANES - nope), w_uk.dtype)], axis=-1)
    lane = jnp.arange(LANES)
    live = (lane >= nope) & (lane < nope + rope)
    place = jnp.where(live[:, None], jnp.eye(LANES, dtype=F32), 0.0)
    wk_rope = jnp.broadcast_to(place[:, None, :], (LANES, n_mla, LANES))
    wk = jnp.concatenate([wk_nope, wk_rope], axis=0).reshape(kv_rank + LANES, n_mla * LANES).astype(BF16)
    assert n_mla % 2 == 0 and 2 * v_dim == LANES
    wv = w_uv.reshape(kv_rank, n_mla * v_dim).astype(BF16)
    wuk_t = jnp.transpose(w_uk, (1, 2, 0)).astype(BF16)
    wuv_h = jnp.transpose(w_uv, (1, 0, 2)).astype(BF16)
    return w1, cols, wqq, wk, wv, wuk_t, wuv_h, nope, v_dim, n_mla


def kernel(x_prompt, x_sample, cache_mla_ckv, cache_mla_krope, state_ret, state_ffn_conv, cache_mem_k,
           cache_mem_v, mem_prompt, norm_mix_g, w_in, q_norm_g, kv_norm_g, w_uq, w_uk, w_uv, ret_gn_g, w_o,
           norm_mem_g, mem_norm_g, w_cq, w_ck, w_cv, w_co, norm_ffn_g, w_up, conv_w, conv_b, w_down,
           final_norm_g):
    bp, tp, d = x_prompt.shape
    bs, ts, _ = x_sample.shape
    depth = w_in.shape[0]
    past = cache_mla_ckv.shape[2]
    q_rank, kv_rank = w_uq.shape[1], w_uk.shape[1]
    rope = cache_mla_krope.shape[3]
    n_ret, ret_dk, ret_dv = state_ret.shape[2:]
    mem_tok, mem_heads, mem_dh = cache_mem_k.shape[2:]
    ff2 = w_up.shape[2]
    assert conv_w.shape[1] == 3 and ret_dk == LANES
    qk = w_uq.shape[3]
    q_scale = float(qk) ** -0.5 * LOG2E
    log_g = tuple(math.log(1.0 - 2.0 ** (-5.0 - i)) for i in range(n_ret))

    tile_i = math.gcd(PROMPT_TILE, tp)
    tile_s = min(ROW_TILE, bs * ts)
    assert tile_s % ts == 0
    tabs_p = _rope_tables(jnp.arange(0, tp, tile_i), jnp.arange(tile_i), rope)
    tabs_s = _rope_tables(jnp.full((1,), past), jnp.arange(tile_s) % ts, rope)

    hp = x_prompt.reshape(bp * tp, d)
    hs = x_sample.reshape(bs * ts, d)
    row2 = lambda a: a.reshape(1, -1)
    outs = {k: [] for k in ("p_ckv", "p_kr", "p_ret", "p_conv", "p_mk", "p_mv", "s_ckv", "s_kr", "s_ret", "s_conv")}
    for l in range(depth):
        w1, cols, wqq, wk, wv, wuk_t, wuv_h, nope, v_dim, n_mla = _layer_weights(
            w_in[l], w_uq[l], w_uk[l], w_uv[l], q_rank, kv_rank, rope, n_ret)
        inproj = functools.partial(
            _inproj, g=row2(norm_mix_g[l]), w1=w1, qg=row2(q_norm_g[l]), kvg=row2(kv_norm_g[l]), wqq=wqq,
            wk=wk, wv=wv, cols=cols, n_mla=n_mla, n_ret=n_ret, nope=nope, rope=rope, kv_rank=kv_rank,
            v_dim=v_dim, q_scale=q_scale)
        wo = w_o[l].astype(BF16)
        wcq = w_cq[l].reshape(d, mem_heads * mem_dh).astype(BF16)
        wco = w_co[l].astype(BF16)
        wckv = jnp.concatenate([w_ck[l].reshape(d, -1), w_cv[l].reshape(d, -1)], axis=1).astype(BF16)
        wup = w_up[l].astype(BF16)
        wdn = w_down[l].astype(BF16)
        gn = row2(ret_gn_g[l])
        mix = functools.partial(_mix, wo=wo, g=row2(norm_mem_g[l]), wcq=wcq, wco=wco, n_heads=mem_heads)
        ffn = functools.partial(_ffn, g=row2(norm_ffn_g[l]), wup=wup, cw=conv_w[l], cb=row2(conv_b[l]),
                                wdn=wdn, gfin=row2(final_norm_g))

        q, k, v, ckv, kr, rq, rk, rv, rg = inproj(hp, tabs=tabs_p, q_transposed=True)
        o_mla = _mla_attn(q, k.reshape(bp, tp, -1), v, v_dim=v_dim)
        b3 = lambda a: a.reshape(bp, tp, -1)
        o_ret, ret_fin = _retention(b3(rq), b3(rk), b3(rv), b3(rg),
                                    jnp.zeros((bp, n_ret, ret_dk, ret_dv), F32), gn, log_g)
        mk, mv = _memkv(mem_prompt.reshape(bp * mem_tok, d), row2(mem_norm_g[l]), wckv)
        x2 = mix(hp, o_mla.reshape(bp * tp, -1), o_ret.reshape(bp * tp, -1),
                 mk=mk.reshape(bp, mem_tok, -1), mv=mv.reshape(bp, mem_tok, -1), bb=1, tt=tile_i)
        hp_next, conv_fin = ffn(x2, state=jnp.zeros((bp, 2, ff2), F32), bb=1, tt=tile_i)
        outs["p_ckv"].append(ckv.reshape(bp, tp, kv_rank))
        outs["p_kr"].append(kr.reshape(bp, tp, rope))
        outs["p_ret"].append(ret_fin)
        outs["p_conv"].append(conv_fin)
        outs["p_mk"].append(mk.reshape(bp, mem_tok, mem_heads, mem_dh))
        outs["p_mv"].append(mv.reshape(bp, mem_tok, mem_heads, mem_dh))

        q, k, v, ckv, kr, rq, rk, rv, rg = inproj(hs, tabs=tabs_s, q_transposed=False)
        o_mla = _dec_attn(q.reshape(bs, ts, -1), cache_mla_ckv[l], jnp.swapaxes(cache_mla_krope[l], 1, 2),
                          ckv.reshape(bs, ts, -1), kr.reshape(bs, ts, -1), wuk_t, wuv_h,
                          nope=nope, rope=rope, v_dim=v_dim)
        s3 = lambda a: a.reshape(bs, ts, -1)
        o_ret, ret_fin = _retention(s3(rq), s3(rk), s3(rv), s3(rg), state_ret[l], gn, log_g)
        bb = math.gcd(bs, DEC_MEM_GROUP)
        x2 = mix(hs, o_mla.reshape(bs * ts, -1), o_ret.reshape(bs * ts, -1),
                 mk=cache_mem_k[l].reshape(bs, mem_tok, -1), mv=cache_mem_v[l].reshape(bs, mem_tok, -1),
                 bb=bb, tt=ts)
        hs_next, conv_fin = ffn(x2, state=state_ffn_conv[l], bb=bs, tt=ts)
        outs["s_ckv"].append(ckv.reshape(bs, ts, kv_rank))
        outs["s_kr"].append(kr.reshape(bs, ts, rope))
        outs["s_ret"].append(ret_fin)
        outs["s_conv"].append(conv_fin)
        assert depth == 1
        hp, hs = hp_next, hs_next

    st = lambda name: jnp.stack(outs[name])
    return (hp.reshape(bp, tp, d), hs.reshape(bs, ts, d),
            st("p_ckv"), st("p_kr"), st("p_ret"), st("p_conv"), st("p_mk"), st("p_mv"),
            st("s_ckv"), st("s_kr"), st("s_ret"), st("s_conv"))
```

```python
import functools
import math

import jax
import jax.numpy as jnp
from jax import lax
from jax.experimental import pallas as pl
from jax.experimental.pallas import tpu as pltpu

F32 = jnp.float32
BF16 = jnp.bfloat16

EPS = 1e-6
ROPE_BASE = 10000.0
MLA_CHUNK = 64
LOG2E = 1.4426950408889634
LANES = 128
SUBLANES = 8
NEG_BIG = -1e30

ROW_TILE = 512
PROMPT_TILE = 1024
ATTN_Q_TILE = 2048
ATTN_K_TILE = 512
RET_BLOCK = 1024
RET_CHUNK = 256
FF_CHUNK = 256
DEC_MEM_GROUP = 8
VMEM_LIMIT = 56 * 1024 * 1024


def _nt_dot(a, b):
    return lax.dot_general(a, b, (((1,), (1,)), ((), ())), preferred_element_type=F32)


def _dot(a, b):
    return jnp.dot(a, b, preferred_element_type=F32)


def _rms(x, g):
    return x * lax.rsqrt(jnp.mean(x * x, axis=-1, keepdims=True) + EPS) * g


def _silu(x):
    return x / (1.0 + jnp.exp(-x))


def _params(sem):
    return pltpu.CompilerParams(dimension_semantics=sem, vmem_limit_bytes=VMEM_LIMIT)


def _inproj_kernel(x_ref, g_ref, w1_ref, qg_ref, kvg_ref, wqq_ref, wk_ref, wv_ref, base_ref, off_ref, *rest,
                   cols, n_mla, n_ret, nope, rope, v_dim, q_scale, rk_scale, ret):
    if ret is None:
        q_out, k_out, v_out, ckv_out, kr_out, rq_out, rk_out, rv_out, rg_out = rest
    else:
        (s0_ref, gn_ref, q_out, k_out, v_out, ckv_out, kr_out, oret_out, sfin_out,
         rq_out, rk_out, rv_out, rg_out, s_sc) = rest
    o_q, o_kv, o_kr, o_rq, o_rk, o_rv, o_rg, o_end = cols
    h = _rms(x_ref[...], g_ref[...]).astype(BF16)

    ca, sa, cra, sra = (base_ref[0, i:i + 1, :] for i in range(4))
    cb, sb, crb, srb = (off_ref[i] for i in range(4))
    cq_t, sq_t = ca * cb - sa * sb, sa * cb + ca * sb
    cr_t, sr_t = cra * crb - sra * srb, sra * crb + cra * srb

    cq = _rms(_dot(h, w1_ref[:, o_q:o_kv]), qg_ref[...]).astype(BF16)
    qq = _dot(cq, wqq_ref[...])
    for hh in range(n_mla):
        a = qq[:, hh * LANES:(hh + 1) * LANES]
        b = qq[:, (n_mla + hh) * LANES:(n_mla + hh + 1) * LANES]
        slab = (a * cq_t + b * sq_t) * q_scale
        if len(q_out.shape) == 3:
            q_out[0, hh * LANES:(hh + 1) * LANES, :] = slab.T.astype(BF16)
        else:
            q_out[:, hh * LANES:(hh + 1) * LANES] = slab.astype(BF16)

    ckv = _rms(_dot(h, w1_ref[:, o_kv:o_kr]), kvg_ref[...])
    ckv_out[...] = ckv
    zkr = _dot(h, w1_ref[:, o_kr:o_rq])
    kr = zkr * cq_t + pltpu.roll(zkr, LANES - rope, 1) * sq_t
    kr_out[...] = kr[:, nope:nope + rope]
    ckv_b = ckv.astype(BF16)
    k_out[...] = _dot(jnp.concatenate([ckv_b, kr.astype(BF16)], axis=-1), wk_ref[...]).astype(BF16)
    v_nat = _dot(ckv_b, wv_ref[...])
    lane = lax.broadcasted_iota(jnp.int32, (1, LANES), 1)
    one_even, one_odd = (lane == v_dim).astype(F32), (lane == 0).astype(F32)
    slabs = []
    for j in range(n_mla // 2):
        pair = v_nat[:, j * LANES:(j + 1) * LANES]
        slabs += [jnp.where(lane < v_dim, pair, one_even), jnp.where(lane >= v_dim, pair, one_odd)]
    v_slabs = jnp.concatenate(slabs, axis=-1)
    v_t = v_slabs.T.astype(BF16)
    tk = v_out.shape[4]
    for j in range(v_out.shape[2]):
        v_out[0, :, j, :, :] = v_t[:, j * tk:(j + 1) * tk].reshape(n_mla // 2, 2 * LANES, tk)

    zq = _dot(h, w1_ref[:, o_rq:o_rk])
    zk = _dot(h, w1_ref[:, o_rk:o_rv])
    for hh in range(n_ret):
        sl = slice(hh * LANES, (hh + 1) * LANES)
        a = zq[:, sl]
        rq_out[:, sl] = (a * cr_t + pltpu.roll(a, LANES // 2, 1) * sr_t).astype(BF16)
        b = zk[:, sl]
        rk_out[:, sl] = ((b * cr_t + pltpu.roll(b, LANES // 2, 1) * sr_t) * rk_scale).astype(BF16)
    rv_out[...] = _dot(h, w1_ref[:, o_rv:o_rg]).astype(BF16)
    rg_out[...] = _silu(_dot(h, w1_ref[:, o_rg:o_end])).astype(BF16)

    if ret is not None:
        lc, log_g, seq_tiles = ret
        ti = pl.program_id(0) % seq_tiles

        @pl.when(ti == 0)
        def _():
            s_sc[...] = s0_ref[0]

        _ret_chunks(rq_out, rk_out, rv_out, rg_out, gn_ref, oret_out, s_sc, (), lc=lc,
                    n_sub=rq_out.shape[0] // lc, log_g=log_g)

        @pl.when(ti == seq_tiles - 1)
        def _():
            sfin_out[0] = s_sc[...]


def _inproj(x2d, g, w1, qg, kvg, wqq, wk, wv, tabs, *, cols, n_mla, n_ret, nope, rope, kv_rank, v_dim,
            q_scale, q_transposed, ret=None):
    rows, d = x2d.shape
    base_tab, off_tab = tabs
    tm = off_tab.shape[1]
    assert rows % tm == 0
    base_tiles = base_tab.shape[0]
    tk = math.gcd(ATTN_K_TILE, tm)
    ret_w = n_ret * LANES
    row = lambda w: pl.BlockSpec((tm, w), lambda i: (i, 0))
    full = lambda a: pl.BlockSpec(a.shape, lambda i: (0,) * a.ndim)
    base = pl.BlockSpec((1,) + base_tab.shape[1:], lambda i: (i % base_tiles, 0, 0))
    n_seq = rows // (tm * base_tiles)
    out_shape = (
        jax.ShapeDtypeStruct((n_seq, n_mla * LANES, tm * base_tiles) if q_transposed
                             else (rows, n_mla * LANES), BF16),
        jax.ShapeDtypeStruct((rows, n_mla * LANES), BF16),
        jax.ShapeDtypeStruct((rows // (tm * base_tiles), n_mla // 2, base_tiles * (tm // tk), 2 * LANES, tk),
                             BF16),
        jax.ShapeDtypeStruct((rows, kv_rank), F32),
        jax.ShapeDtypeStruct((rows, rope), F32),
        jax.ShapeDtypeStruct((rows, ret_w), BF16),
        jax.ShapeDtypeStruct((rows, ret_w), BF16),
        jax.ShapeDtypeStruct((rows, ret_w), BF16),
        jax.ShapeDtypeStruct((rows, ret_w), BF16),
    )
    args = [x2d, g, w1, qg, kvg, wqq, wk, wv, base_tab, off_tab]
    in_specs = [row(d), full(g), full(w1), full(qg), full(kvg), full(wqq), full(wk), full(wv), base,
                full(off_tab)]
    out_specs = [
        row(s.shape[1]) if len(s.shape) == 2 else
        pl.BlockSpec((1, s.shape[1], tm), lambda i: (i // base_tiles, 0, i % base_tiles))
        if len(s.shape) == 3 else
        pl.BlockSpec((1, s.shape[1], tm // tk) + s.shape[3:],
                     lambda i: (i // base_tiles, 0, i % base_tiles, 0, 0))
        for s in out_shape]
    scratch, ret_static = [], None
    if ret is not None:
        s0, gn_g, log_g = ret
        lc = min(RET_CHUNK, tm)
        assert s0.shape[0] == n_seq and tm % lc == 0
        state = pl.BlockSpec((1,) + s0.shape[1:], lambda i: (i // base_tiles, 0, 0, 0))
        args += [s0, gn_g]
        in_specs += [state, full(gn_g)]
        out_shape = out_shape[:5] + (jax.ShapeDtypeStruct((rows, ret_w), BF16),
                                     jax.ShapeDtypeStruct(s0.shape, F32))
        out_specs = out_specs[:5] + [row(ret_w), state]
        scratch = [pltpu.VMEM((tm, ret_w), BF16)] * 4 + [pltpu.VMEM(s0.shape[1:], F32)]
        ret_static = (lc, log_g, base_tiles)
    kern = functools.partial(_inproj_kernel, cols=cols, n_mla=n_mla, n_ret=n_ret, nope=nope, rope=rope,
                             v_dim=v_dim, q_scale=q_scale, rk_scale=float(LANES) ** -0.5, ret=ret_static)
    return pl.pallas_call(
        kern,
        grid=(rows // tm,),
        in_specs=in_specs,
        out_specs=tuple(out_specs),
        out_shape=out_shape,
        scratch_shapes=scratch,
        compiler_params=_params(("arbitrary",)),
        name="inproj",
    )(*args)


def _mla_attn_kernel(qt_ref, k_ref, vt_ref, o_ref, m_sc, a_sc, *, tq, tk, v_dim):
    qi = pl.program_id(2)
    m_sc[...] = jnp.full(m_sc.shape, -jnp.inf, F32)
    a_sc[...] = jnp.zeros(a_sc.shape, F32)

    def step(kt, c0, c1, bias):
        ks = pl.multiple_of(kt * tk, tk)
        for hh in range(2):
            sl = slice(hh * LANES, (hh + 1) * LANES)
            s = _dot(k_ref[0, pl.ds(ks, tk), sl], qt_ref[0, sl, c0:c1])
            if bias is not None:
                s = s + bias
            m_old = m_sc[hh, :, c0:c1]
            m_new = jnp.maximum(m_old, jnp.max(s, axis=0, keepdims=True))
            p = jnp.exp2(s - jnp.tile(m_new, (tk // SUBLANES, 1)))
            alpha = jnp.tile(jnp.exp2(m_old - m_new), (LANES // SUBLANES, 1))
            a_sc[hh, :, c0:c1] = alpha * a_sc[hh, :, c0:c1] + _dot(vt_ref[0, 0, kt, sl, :], p.astype(BF16))
            m_sc[hh, :, c0:c1] = m_new

    n_diag = tq // tk
    n_full = qi * n_diag

    def body(j, c):
        for i in range(n_diag):
            step(j * n_diag + i, 0, tq, None)
        return c

    lax.fori_loop(0, qi, body, 0)
    key_c = lax.broadcasted_iota(jnp.int32, (tk, tq), 0) // MLA_CHUNK
    qry_c = lax.broadcasted_iota(jnp.int32, (tk, tq), 1) // MLA_CHUNK
    diag_bias = jnp.where(key_c <= qry_c, 0.0, NEG_BIG)
    for i in range(n_diag):
        step(n_full + i, i * tk, tq, diag_bias[:, :tq - i * tk])
    a0, a1 = a_sc[0], a_sc[1]
    o_t = jnp.concatenate([a0[:v_dim] / a0[v_dim:v_dim + 1], a1[v_dim:] / a1[0:1]], axis=0)
    o_ref[0] = o_t.T.astype(BF16)


def _mla_attn(qt, k, vt, *, v_dim):
    b, t, hw = k.shape
    assert qt.shape == (b, hw, t)
    n_pair = hw // (2 * LANES)
    tk = vt.shape[4]
    tq = min(ATTN_Q_TILE, t)
    assert 2 * v_dim == LANES and vt.shape == (b, n_pair, t // tk, 2 * LANES, tk)
    assert t % tq == 0 and tq % tk == 0 and tk % MLA_CHUNK == 0
    kern = functools.partial(_mla_attn_kernel, tq=tq, tk=tk, v_dim=v_dim)
    return pl.pallas_call(
        kern,
        grid=(b, n_pair, t // tq),
        in_specs=[pl.BlockSpec((1, 2 * LANES, tq), lambda bi, j, i: (bi, j, i)),
                  pl.BlockSpec((1, t, 2 * LANES), lambda bi, j, i: (bi, 0, j)),
                  pl.BlockSpec((1, 1) + vt.shape[2:], lambda bi, j, i: (bi, j, 0, 0, 0))],
        out_specs=pl.BlockSpec((1, tq, LANES), lambda bi, j, i: (bi, i, j)),
        out_shape=jax.ShapeDtypeStruct((b, t, n_pair * LANES), BF16),
        scratch_shapes=[pltpu.VMEM((2, SUBLANES, tq), F32), pltpu.VMEM((2, LANES, tq), F32)],
        compiler_params=_params(("parallel", "parallel", "arbitrary")),
        name="mla_prompt_attn",
    )(qt, k, vt)


def _dec_attn_kernel(q_ref, cc_ref, kc_ref, cn_ref, kn_ref, wuk_ref, wuv_ref, o_ref, qa_sc, qr_sc,
                     *, n_heads, ts, past, nope, rope, v_dim):
    for hh in range(n_heads):
        qh = q_ref[0, :, hh * LANES:(hh + 1) * LANES]
        qa_sc[hh * ts:(hh + 1) * ts, :] = _dot(qh[:, :nope], wuk_ref[hh]).astype(BF16)
        qr_sc[hh * ts:(hh + 1) * ts, :] = qh[:, nope:nope + rope]
    qa, qr = qa_sc[...], qr_sc[...]
    cc = cc_ref[0].astype(BF16)
    cn = cn_ref[0].astype(BF16)
    s1 = _nt_dot(qa, cc) + _dot(qr, kc_ref[0].astype(BF16))
    s2 = _nt_dot(qa, cn) + _nt_dot(qr, kn_ref[0].astype(BF16))
    rows = n_heads * ts
    qchunk1 = (past + lax.broadcasted_iota(jnp.int32, (rows, past), 0) % ts) // MLA_CHUNK
    kchunk1 = lax.broadcasted_iota(jnp.int32, (rows, past), 1) // MLA_CHUNK
    s1 = jnp.where(kchunk1 <= qchunk1, s1, NEG_BIG)
    qchunk2 = (past + lax.broadcasted_iota(jnp.int32, (rows, ts), 0) % ts) // MLA_CHUNK
    kchunk2 = (past + lax.broadcasted_iota(jnp.int32, (rows, ts), 1)) // MLA_CHUNK
    s2 = jnp.where(kchunk2 <= qchunk2, s2, NEG_BIG)
    m = jnp.maximum(jnp.max(s1, axis=-1, keepdims=True), jnp.max(s2, axis=-1, keepdims=True))
    p1 = jnp.exp2(s1 - m)
    p2 = jnp.exp2(s2 - m)
    l = jnp.sum(p1, axis=-1, keepdims=True) + jnp.sum(p2, axis=-1, keepdims=True)
    lat = ((_dot(p1.astype(BF16), cc) + _dot(p2.astype(BF16), cn)) / l).astype(BF16)
    per_slab = LANES // v_dim
    for j in range(n_heads // per_slab):
        parts = [_dot(lat[(j * per_slab + i) * ts:(j * per_slab + i + 1) * ts, :], wuv_ref[j * per_slab + i])
                 for i in range(per_slab)]
        o_ref[0, :, j * LANES:(j + 1) * LANES] = jnp.concatenate(parts, axis=-1).astype(BF16)


def _dec_attn(q, ckv_cache, kr_cache, ckv_new, kr_new, wuk_t, wuv_h, *, nope, rope, v_dim):
    b, ts, hw = q.shape
    n_heads = hw // LANES
    past, kv_rank = ckv_cache.shape[1:]
    kern = functools.partial(_dec_attn_kernel, n_heads=n_heads, ts=ts, past=past, nope=nope, rope=rope,
                             v_dim=v_dim)
    per_b = lambda a: pl.BlockSpec((1,) + a.shape[1:], lambda bi: (bi,) + (0,) * (a.ndim - 1))
    full = lambda a: pl.BlockSpec(a.shape, lambda bi: (0,) * a.ndim)
    return pl.pallas_call(
        kern,
        grid=(b,),
        in_specs=[per_b(q), per_b(ckv_cache), per_b(kr_cache), per_b(ckv_new), per_b(kr_new),
                  full(wuk_t), full(wuv_h)],
        out_specs=pl.BlockSpec((1, ts, n_heads * v_dim), lambda bi: (bi, 0, 0)),
        out_shape=jax.ShapeDtypeStruct((b, ts, n_heads * v_dim), BF16),
        scratch_shapes=[pltpu.VMEM((n_heads * ts, kv_rank), BF16), pltpu.VMEM((n_heads * ts, rope), BF16)],
        compiler_params=_params(("parallel",)),
        name="mla_decode_attn",
    )(q, ckv_cache, kr_cache, ckv_new, kr_new, wuk_t, wuv_h)


def _ret_chunks(rq_ref, rk_ref, rv_ref, gt_ref, g_ref, o_ref, s_sc, lead, *, lc, n_sub, log_g):
    diff = (lax.broadcasted_iota(jnp.int32, (lc, lc), 0)
            - lax.broadcasted_iota(jnp.int32, (lc, lc), 1)).astype(F32)
    pos = lax.broadcasted_iota(jnp.int32, (lc, 1), 0).astype(F32)
    for hh, lg in enumerate(log_g):
        sl = slice(hh * LANES, (hh + 1) * LANES)
        dmask = jnp.where(diff >= 0, jnp.exp(lg * jnp.maximum(diff, 0.0)), 0.0)
        q_dec = jnp.exp(lg * (pos + 1.0))
        k_dec = jnp.exp(lg * (lc - 1.0 - pos))
        s_dec = math.exp(lg * lc)
        gain = g_ref[:, sl]
        for c in range(n_sub):
            rows = slice(c * lc, (c + 1) * lc)
            at = lead + (rows, sl)
            q, k, v = rq_ref[at], rk_ref[at], rv_ref[at]
            s_prev = s_sc[hh]
            inner = _nt_dot(q, k) * dmask
            o = _dot(inner.astype(BF16), v)
            o = o + _dot((q.astype(F32) * q_dec).astype(BF16), s_prev.astype(BF16))
            kd = (k.astype(F32) * k_dec).astype(BF16)
            s_sc[hh] = s_dec * s_prev + lax.dot_general(kd, v, (((0,), (0,)), ((), ())),
                                                        preferred_element_type=F32)
            mu = jnp.mean(o, axis=-1, keepdims=True)
            oc = o - mu
            var = jnp.mean(oc * oc, axis=-1, keepdims=True)
            y = oc * lax.rsqrt(var + EPS) * gain * gt_ref[at].astype(F32)
            o_ref[at] = y.astype(BF16)


def _ret_kernel(rq_ref, rk_ref, rv_ref, gt_ref, s0_ref, g_ref, o_ref, sfin_ref, s_sc,
                *, lc, n_sub, log_g):
    t = pl.program_id(1)

    @pl.when(t == 0)
    def _():
        s_sc[...] = s0_ref[0]

    _ret_chunks(rq_ref, rk_ref, rv_ref, gt_ref, g_ref, o_ref, s_sc, (0,), lc=lc, n_sub=n_sub, log_g=log_g)

    @pl.when(t == pl.num_programs(1) - 1)
    def _():
        sfin_ref[0] = s_sc[...]


def _retention(rq, rk, rv, gate, s0, gn_g, log_g):
    b, t, w = rq.shape
    n_heads, dk, dv = s0.shape[1:]
    lb = min(RET_BLOCK, t)
    lc = min(RET_CHUNK, lb)
    assert t % lb == 0 and lb % lc == 0 and dk == LANES and dv == LANES
    kern = functools.partial(_ret_kernel, lc=lc, n_sub=lb // lc, log_g=log_g)
    blk = pl.BlockSpec((1, lb, w), lambda bi, ti: (bi, ti, 0))
    st = pl.BlockSpec((1, n_heads, dk, dv), lambda bi, ti: (bi, 0, 0, 0))
    return pl.pallas_call(
        kern,
        grid=(b, t // lb),
        in_specs=[blk, blk, blk, blk, st, pl.BlockSpec(gn_g.shape, lambda bi, ti: (0, 0))],
        out_specs=(blk, st),
        out_shape=(jax.ShapeDtypeStruct((b, t, w), BF16), jax.ShapeDtypeStruct(s0.shape, F32)),
        scratch_shapes=[pltpu.VMEM((n_heads, dk, dv), F32)],
        compiler_params=_params(("parallel", "arbitrary")),
        name="retention",
    )(rq, rk, rv, gate, s0, gn_g)


def _memkv_kernel(m_ref, g_ref, w_ref, k_out, v_out):
    h = _rms(m_ref[...], g_ref[...]).astype(BF16)
    n = k_out.shape[1]
    k_out[...] = _dot(h, w_ref[:, :n])
    v_out[...] = _dot(h, w_ref[:, n:])


def _memkv(mem2d, g, wkv):
    rows, d = mem2d.shape
    n = wkv.shape[1] // 2
    tm = min(ROW_TILE, rows)
    assert rows % tm == 0
    row = lambda w: pl.BlockSpec((tm, w), lambda i: (i, 0))
    full = lambda a: pl.BlockSpec(a.shape, lambda i: (0,) * a.ndim)
    return pl.pallas_call(
        _memkv_kernel,
        grid=(rows // tm,),
        in_specs=[row(d), full(g), full(wkv)],
        out_specs=(row(n), row(n)),
        out_shape=(jax.ShapeDtypeStruct((rows, n), F32), jax.ShapeDtypeStruct((rows, n), F32)),
        compiler_params=_params(("parallel",)),
        name="mem_kv",
    )(mem2d, g, wkv)


def _mix_kernel(x_ref, om_ref, or_ref, wo_ref, g_ref, wcq_ref, wco_ref, mk_ref, mv_ref, o_ref, qc_sc, oc_sc,
                *, bb, tt, n_heads, dh):
    mixed = jnp.concatenate([om_ref[...], or_ref[...]], axis=-1)
    x1 = x_ref[...] + _dot(mixed, wo_ref[...])
    h = _rms(x1, g_ref[...]).astype(BF16)
    qc_sc[...] = (_dot(h, wcq_ref[...]) * (float(dh) ** -0.5)).astype(BF16)
    for bi in range(bb):
        rows = slice(bi * tt, (bi + 1) * tt)
        for hh in range(n_heads):
            sl = slice(hh * dh, (hh + 1) * dh)
            s = _nt_dot(qc_sc[rows, sl], mk_ref[bi, :, sl].astype(BF16))
            p = jnp.exp(s - jnp.max(s, axis=-1, keepdims=True))
            l = jnp.sum(p, axis=-1, keepdims=True)
            o = _dot(p.astype(BF16), mv_ref[bi, :, sl].astype(BF16)) / l
            oc_sc[rows, sl] = o.astype(BF16)
    o_ref[...] = x1 + _dot(oc_sc[...], wco_ref[...])


def _mix(x2d, om, orr, wo, g, wcq, mk, mv, wco, *, bb, tt, n_heads):
    rows, d = x2d.shape
    b = mk.shape[0]
    r = bb * tt
    n_t = rows // (b * tt)
    assert rows % r == 0 and b % bb == 0 and (bb == 1 or n_t == 1) and mk.shape == mv.shape
    row = lambda w: pl.BlockSpec((r, w), lambda gi, ti: (gi * n_t + ti, 0))
    full = lambda a: pl.BlockSpec(a.shape, lambda gi, ti: (0,) * a.ndim)
    mem = pl.BlockSpec((bb,) + mk.shape[1:], lambda gi, ti: (gi,) + (0,) * (mk.ndim - 1))
    kern = functools.partial(_mix_kernel, bb=bb, tt=tt, n_heads=n_heads, dh=d // n_heads)
    return pl.pallas_call(
        kern,
        grid=(b // bb, n_t),
        in_specs=[row(d), row(om.shape[1]), row(orr.shape[1]), full(wo), full(g), full(wcq), full(wco),
                  mem, mem],
        out_specs=row(d),
        out_shape=jax.ShapeDtypeStruct((rows, d), F32),
        scratch_shapes=[pltpu.VMEM((r, d), BF16), pltpu.VMEM((r, d), BF16)],
        compiler_params=_params(("parallel", "arbitrary")),
        name="mix_out_mem_attn",
    )(x2d, om, orr, wo, g, wcq, wco, mk, mv)


def _ffn_kernel(x_ref, g_ref, wup_ref, cw_ref, cb_ref, wdn_ref, st_ref, gf_ref, y_ref, st_out, carry,
                ext_a, ext_g, *, bb, tt, ff, fc):
    t = pl.program_id(1)

    @pl.when(t == 0)
    def _():
        carry[...] = st_ref[...]

    r = bb * tt
    x = x_ref[...]
    h = _rms(x, g_ref[...]).astype(BF16)
    halo = ext_a.shape[1] - tt

    def conv(cs, ext):
        u = _dot(h, wup_ref[:, cs])
        ext[:, halo:, :] = u.reshape(bb, tt, fc)
        ext[:, halo - 2:halo, :] = carry[:, :, cs]
        um2 = ext[:, halo - 2:halo - 2 + tt, :].reshape(r, fc)
        um1 = ext[:, halo - 1:halo - 1 + tt, :].reshape(r, fc)
        carry[:, :, cs] = ext[:, halo + tt - 2:, :]
        return cb_ref[:, cs] + cw_ref[0:1, cs] * um2 + cw_ref[1:2, cs] * um1 + cw_ref[2:3, cs] * u

    acc = jnp.zeros(x.shape, F32)
    for c in range(ff // fc):
        a = conv(slice(c * fc, (c + 1) * fc), ext_a)
        gte = conv(slice(ff + c * fc, ff + (c + 1) * fc), ext_g)
        acc = acc + _dot((_silu(a) * gte).astype(BF16), wdn_ref[c * fc:(c + 1) * fc, :])
    y_ref[...] = _rms(x + acc, gf_ref[...])
    st_out[...] = carry[...]


SEG = SUBLANES * SUBLANES


def _ffn_seq_kernel(x_ref, g_ref, wup_ref, cw_ref, cb_ref, wdn_ref, st_ref, gf_ref, y_ref, st_out, carry,
                    slab, act, *, tt, ff, fc):
    t = pl.program_id(1)
    d = x_ref.shape[1]
    n_seg, n_slab = tt // SEG, d // LANES

    @pl.when(t == 0)
    def _():
        for j in range(2):
            carry[j] = jnp.broadcast_to(st_ref[0, j:j + 1, :], carry.shape[1:])

    for k in range(n_slab):
        for c in range(n_seg):
            for a in range(SUBLANES):
                src = x_ref[SEG * c + SUBLANES * a:SEG * c + SUBLANES * (a + 1), k * LANES:(k + 1) * LANES]
                slab[k, pl.ds(SEG * c + a, SUBLANES, stride=SUBLANES), :] = src
    x = jnp.concatenate([slab[k] for k in range(n_slab)], axis=-1)
    h = _rms(x, g_ref[...]).astype(BF16)
    sub0 = lax.broadcasted_iota(jnp.int32, (n_seg, SUBLANES, fc), 1) == 0

    def conv(cs):
        u = _dot(h, wup_ref[:, cs]).reshape(n_seg, SUBLANES, SUBLANES, fc)

        def shifted(j):
            prev = jnp.concatenate([carry[j - 6, :, cs][None], u[:-1, j]], axis=0)
            return jnp.where(sub0, pltpu.roll(prev, 1, 1), pltpu.roll(u[:, j], 1, 1))[:, None]

        r6, r7 = shifted(6), shifted(7)
        tap1 = jnp.concatenate([r7, u[:, :SUBLANES - 1]], axis=1)
        tap2 = jnp.concatenate([r6, r7, u[:, :SUBLANES - 2]], axis=1)
        carry[0, :, cs] = u[n_seg - 1, 6]
        carry[1, :, cs] = u[n_seg - 1, 7]
        cv = cb_ref[:, cs] + cw_ref[0:1, cs] * tap2 + cw_ref[1:2, cs] * tap1 + cw_ref[2:3, cs] * u
        return cv.reshape(tt, fc)

    for c in range(ff // fc):
        a = conv(slice(c * fc, (c + 1) * fc))
        gte = conv(slice(ff + c * fc, ff + (c + 1) * fc))
        act[:, c * fc:(c + 1) * fc] = (_silu(a) * gte).astype(BF16)
    y = _rms(x + _dot(act[...], wdn_ref[...]), gf_ref[...])

    for k in range(n_slab):
        slab[k] = y[:, k * LANES:(k + 1) * LANES]
    for k in range(n_slab):
        for c in range(n_seg):
            for a in range(SUBLANES):
                y_ref[SEG * c + SUBLANES * a:SEG * c + SUBLANES * (a + 1), k * LANES:(k + 1) * LANES] = (
                    slab[k, pl.ds(SEG * c + a, SUBLANES, stride=SUBLANES), :])
    for j in range(2):
        st_out[0, j:j + 1, :] = carry[j, SUBLANES - 1:SUBLANES, :]


def _ffn(x2d, g, wup, cw, cb, wdn, state, gfin, *, bb, tt):
    rows, d = x2d.shape
    b, n_keep, ff2 = state.shape
    assert n_keep == 2 and tt >= 2
    ff = ff2 // 2
    fc = FF_CHUNK
    assert ff % fc == 0
    r = bb * tt
    n_t = rows // (b * tt)
    assert rows % r == 0 and b % bb == 0 and (bb == 1 or n_t == 1)
    row = pl.BlockSpec((r, d), lambda gi, ti: (gi * n_t + ti, 0))
    full = lambda a: pl.BlockSpec(a.shape, lambda gi, ti: (0,) * a.ndim)
    st = pl.BlockSpec((bb, n_keep, ff2), lambda gi, ti: (gi, 0, 0))
    if bb == 1 and tt % SEG == 0 and d % LANES == 0:
        kern = functools.partial(_ffn_seq_kernel, tt=tt, ff=ff, fc=fc)
        scratch = [pltpu.VMEM((n_keep, SUBLANES, ff2), F32), pltpu.VMEM((d // LANES, tt, LANES), F32),
                   pltpu.VMEM((tt, ff), BF16)]
    else:
        kern = functools.partial(_ffn_kernel, bb=bb, tt=tt, ff=ff, fc=fc)
        scratch = [pltpu.VMEM((bb, n_keep, ff2), F32),
                   pltpu.VMEM((bb, SUBLANES + tt, fc), F32), pltpu.VMEM((bb, SUBLANES + tt, fc), F32)]
    return pl.pallas_call(
        kern,
        grid=(b // bb, n_t),
        in_specs=[row, full(g), full(wup), full(cw), full(cb), full(wdn), st, full(gfin)],
        out_specs=(row, st),
        out_shape=(jax.ShapeDtypeStruct((rows, d), F32), jax.ShapeDtypeStruct(state.shape, F32)),
        scratch_shapes=scratch,
        compiler_params=_params(("parallel", "arbitrary")),
        name="conv_ffn",
    )(x2d, g, wup, cw, cb, wdn, state, gfin)


def _rope_tables(base_pos, off_pos, rope):
    nope = LANES - 2 * rope
    mf = 1.0 / (ROPE_BASE ** (jnp.arange(0, rope, 2, dtype=F32) / rope))
    f_mla = jnp.concatenate([jnp.zeros((nope,), F32), mf, mf, jnp.zeros((rope,), F32)])
    rf = 1.0 / (ROPE_BASE ** jnp.linspace(0.0, 1.0, LANES // 2, dtype=F32))
    f_ret = jnp.concatenate([rf, rf])
    sign = jnp.concatenate([-jnp.ones((LANES // 2,), F32), jnp.ones((LANES // 2,), F32)])

    def tab(pos):
        p = pos.astype(F32)[:, None]
        return jnp.stack([jnp.cos(p * f_mla), jnp.sin(p * f_mla), jnp.cos(p * f_ret), sign * jnp.sin(p * f_ret)])

    return jnp.transpose(tab(base_pos), (1, 0, 2)), tab(off_pos)


def _layer_weights(w_in, w_uq, w_uk, w_uv, q_rank, kv_rank, rope, n_ret):
    d = w_in.shape[0]
    n_mla, qk = w_uq.shape[1:]
    nope = qk - rope
    v_dim = w_uv.shape[2]
    half = rope // 2
    assert nope + 2 * rope == LANES and kv_rank == LANES and w_uk.shape[2] == nope
    ret_w = n_ret * LANES
    o_kv, o_kr = q_rank, q_rank + kv_rank
    o_rq = o_kr + rope
    assert w_in.shape[1] == o_rq + 4 * ret_w
    wkr = w_in[:, o_kr:o_rq]
    wkr_rot = jnp.concatenate([-wkr[:, half:], wkr[:, :half]], axis=1)
    zl = jnp.zeros((d, nope), w_in.dtype)
    w1 = jnp.concatenate([w_in[:, :o_kr], zl, wkr, wkr_rot, w_in[:, o_rq:]], axis=1).astype(BF16)
    c0 = o_kr + LANES
    cols = (0, o_kv, o_kr, c0, c0 + ret_w, c0 + 2 * ret_w, c0 + 3 * ret_w, c0 + 4 * ret_w)

    zq = lambda w: jnp.zeros((q_rank, n_mla, w), w_uq.dtype)
    r1, r2 = w_uq[..., nope:nope + half], w_uq[..., nope + half:]
    wq = jnp.concatenate([w_uq, zq(rope)], axis=-1).reshape(q_rank, n_mla * LANES)
    wq_rot = jnp.concatenate([zq(nope), -r2, r1, zq(rope)], axis=-1).reshape(q_rank, n_mla * LANES)
    wqq = jnp.concatenate([wq, wq_rot], axis=1).astype(BF16)

    wk_nope = jnp.concatenate([w_uk, jnp.zeros((kv_rank, n_mla, LANES - nope), w_uk.dtype)], axis=-1)
    lane = jnp.arange(LANES)
    live = (lane >= nope) & (lane < nope + rope)
    place = jnp.where(live[:, None], jnp.eye(LANES, dtype=F32), 0.0)
    wk_rope = jnp.broadcast_to(place[:, None, :], (LANES, n_mla, LANES))
    wk = jnp.concatenate([wk_nope, wk_rope], axis=0).reshape(kv_rank + LANES, n_mla * LANES).astype(BF16)
    assert n_mla % 2 == 0 and 2 * v_dim == LANES
    wv = w_uv.reshape(kv_rank, n_mla * v_dim).astype(BF16)
    wuk_t = jnp.transpose(w_uk, (1, 2, 0)).astype(BF16)
    wuv_h = jnp.transpose(w_uv, (1, 0, 2)).astype(BF16)
    return w1, cols, wqq, wk, wv, wuk_t, wuv_h, nope, v_dim, n_mla


def kernel(x_prompt, x_sample, cache_mla_ckv, cache_mla_krope, state_ret, state_ffn_conv, cache_mem_k,
           cache_mem_v, mem_prompt, norm_mix_g, w_in, q_norm_g, kv_norm_g, w_uq, w_uk, w_uv, ret_gn_g, w_o,
           norm_mem_g, mem_norm_g, w_cq, w_ck, w_cv, w_co, norm_ffn_g, w_up, conv_w, conv_b, w_down,
           final_norm_g):
    bp, tp, d = x_prompt.shape
    bs, ts, _ = x_sample.shape
    depth = w_in.shape[0]
    past = cache_mla_ckv.shape[2]
    q_rank, kv_rank = w_uq.shape[1], w_uk.shape[1]
    rope = cache_mla_krope.shape[3]
    n_ret, ret_dk, ret_dv = state_ret.shape[2:]
    mem_tok, mem_heads, mem_dh = cache_mem_k.shape[2:]
    ff2 = w_up.shape[2]
    assert conv_w.shape[1] == 3 and ret_dk == LANES
    qk = w_uq.shape[3]
    q_scale = float(qk) ** -0.5 * LOG2E
    log_g = tuple(math.log(1.0 - 2.0 ** (-5.0 - i)) for i in range(n_ret))

    tile_i = math.gcd(PROMPT_TILE, tp)
    tile_s = min(ROW_TILE, bs * ts)
    assert tile_s % ts == 0
    tabs_p = _rope_tables(jnp.arange(0, tp, tile_i), jnp.arange(tile_i), rope)
    tabs_s = _rope_tables(jnp.full((1,), past), jnp.arange(tile_s) % ts, rope)

    hp = x_prompt.reshape(bp * tp, d)
    hs = x_sample.reshape(bs * ts, d)
    row2 = lambda a: a.reshape(1, -1)
    outs = {k: [] for k in ("p_ckv", "p_kr", "p_ret", "p_conv", "p_mk", "p_mv", "s_ckv", "s_kr", "s_ret", "s_conv")}
    for l in range(depth):
        w1, cols, wqq, wk, wv, wuk_t, wuv_h, nope, v_dim, n_mla = _layer_weights(
            w_in[l], w_uq[l], w_uk[l], w_uv[l], q_rank, kv_rank, rope, n_ret)
        inproj = functools.partial(
            _inproj, g=row2(norm_mix_g[l]), w1=w1, qg=row2(q_norm_g[l]), kvg=row2(kv_norm_g[l]), wqq=wqq,
            wk=wk, wv=wv, cols=cols, n_mla=n_mla, n_ret=n_ret, nope=nope, rope=rope, kv_rank=kv_rank,
            v_dim=v_dim, q_scale=q_scale)
        wo = w_o[l].astype(BF16)
        wcq = w_cq[l].reshape(d, mem_heads * mem_dh).astype(BF16)
        wco = w_co[l].astype(BF16)
        wckv = jnp.concatenate([w_ck[l].reshape(d, -1), w_cv[l].reshape(d, -1)], axis=1).astype(BF16)
        wup = w_up[l].astype(BF16)
        wdn = w_down[l].astype(BF16)
        gn = row2(ret_gn_g[l])
        mix = functools.partial(_mix, wo=wo, g=row2(norm_mem_g[l]), wcq=wcq, wco=wco, n_heads=mem_heads)
        ffn = functools.partial(_ffn, g=row2(norm_ffn_g[l]), wup=wup, cw=conv_w[l], cb=row2(conv_b[l]),
                                wdn=wdn, gfin=row2(final_norm_g))

        q, k, v, ckv, kr, o_ret, ret_fin = inproj(
            hp, tabs=tabs_p, q_transposed=True,
            ret=(jnp.zeros((bp, n_ret, ret_dk, ret_dv), F32), gn, log_g))
        o_mla = _mla_attn(q, k.reshape(bp, tp, -1), v, v_dim=v_dim)
        mk, mv = _memkv(mem_prompt.reshape(bp * mem_tok, d), row2(mem_norm_g[l]), wckv)
        x2 = mix(hp, o_mla.reshape(bp * tp, -1), o_ret.reshape(bp * tp, -1),
                 mk=mk.reshape(bp, mem_tok, -1), mv=mv.reshape(bp, mem_tok, -1), bb=1, tt=tile_i)
        hp_next, conv_fin = ffn(x2, state=jnp.zeros((bp, 2, ff2), F32), bb=1, tt=tile_i)
        outs["p_ckv"].append(ckv.reshape(bp, tp, kv_rank))
        outs["p_kr"].append(kr.reshape(bp, tp, rope))
        outs["p_ret"].append(ret_fin)
        outs["p_conv"].append(conv_fin)
        outs["p_mk"].append(mk.reshape(bp, mem_tok, mem_heads, mem_dh))
        outs["p_mv"].append(mv.reshape(bp, mem_tok, mem_heads, mem_dh))

        q, k, v, ckv, kr, rq, rk, rv, rg = inproj(hs, tabs=tabs_s, q_transposed=False)
        o_mla = _dec_attn(q.reshape(bs, ts, -1), cache_mla_ckv[l], jnp.swapaxes(cache_mla_krope[l], 1, 2),
                          ckv.reshape(bs, ts, -1), kr.reshape(bs, ts, -1), wuk_t, wuv_h,
                          nope=nope, rope=rope, v_dim=v_dim)
        s3 = lambda a: a.reshape(bs, ts, -1)
        o_ret, ret_fin = _retention(s3(rq), s3(rk), s3(rv), s3(rg), state_ret[l], gn, log_g)
        bb = math.gcd(bs, DEC_MEM_GROUP)
        x2 = mix(hs, o_mla.reshape(bs * ts, -1), o_ret.reshape(bs * ts, -1),
                 mk=cache_mem_k[l].reshape(bs, mem_tok, -1), mv=cache_mem_v[l].reshape(bs, mem_tok, -1),
                 bb=bb, tt=ts)
        hs_next, conv_fin = ffn(x2, state=state_ffn_conv[l], bb=bs, tt=ts)
        outs["s_ckv"].append(ckv.reshape(bs, ts, kv_rank))
        outs["s_kr"].append(kr.reshape(bs, ts, rope))
        outs["s_ret"].append(ret_fin)
        outs["s_conv"].append(conv_fin)
        assert depth == 1
        hp, hs = hp_next, hs_next

    st = lambda name: jnp.stack(outs[name])
    return (hp.reshape(bp, tp, d), hs.reshape(bs, ts, d),
            st("p_ckv"), st("p_kr"), st("p_ret"), st("p_conv"), st("p_mk"), st("p_mv"),
            st("s_ckv"), st("s_kr"), st("s_ret"), st("s_conv"))
```
